```python
import math
import jax, jax.numpy as jnp
from jax import lax
import numpy as np

D_MODEL = 1024
BATCH = 8
SEQ = 4096
DEPTH = 4

N_MIXERS = 3
BLOCK = 128
NORM_EPS = 1e-6

SB_HEADS = 16
SB_HEAD_DIM = 64
SB_WIDTH = SB_HEADS * SB_HEAD_DIM

MLA_HEADS = 8
MLA_NOPE_DIM = 128
MLA_ROPE_DIM = 64
MLA_QK_DIM = MLA_NOPE_DIM + MLA_ROPE_DIM
MLA_V_DIM = 128
MLA_Q_RANK = 256
MLA_KV_RANK = 128
MLA_WIDTH = MLA_HEADS * MLA_V_DIM
ROPE_THETA = 10000.0

SWA_HEADS = 16
SWA_KV_HEADS = 4
SWA_GROUP = SWA_HEADS // SWA_KV_HEADS
SWA_HEAD_DIM = 64
SWA_WINDOW = 128
SWA_WIDTH = SWA_HEADS * SWA_HEAD_DIM

kernel_name = 'hybrid_sb_mla_swa_gated_trunk'


def rms_norm(x, g):
    xf = x.astype(jnp.float32)
    y = xf * lax.rsqrt(jnp.mean(xf * xf, axis=-1, keepdims=True) + NORM_EPS)
    return (y * g.astype(jnp.float32)).astype(x.dtype)


def rope(x, pos):
    half = x.shape[-1] // 2
    inv_freq = ROPE_THETA ** (-jnp.arange(half, dtype=jnp.float32) / half)
    ang = pos.astype(jnp.float32)[:, None] * inv_freq[None, :]
    cos = jnp.cos(ang)[None, :, None, :]
    sin = jnp.sin(ang)[None, :, None, :]
    xf = x.astype(jnp.float32)
    x1, x2 = xf[..., :half], xf[..., half:]
    out = jnp.concatenate([x1 * cos - x2 * sin, x2 * cos + x1 * sin], axis=-1)
    return out.astype(x.dtype)


def alibi_slopes(n_heads):
    return 2.0 ** (-8.0 * jnp.arange(1, n_heads + 1, dtype=jnp.float32) / n_heads)


def stick_breaking_attention(q, k, v):
    B, S, H, d = q.shape
    scale = 1.0 / math.sqrt(d)
    outs = []
    for i in range(S // BLOCK):
        t0 = i * BLOCK
        kl = t0 + BLOCK
        z = jnp.einsum('bthd,bshd->bhts', q[:, t0:kl], k[:, :kl]).astype(jnp.float32) * scale
        t_idx = t0 + jnp.arange(BLOCK)[:, None]
        s_idx = jnp.arange(kl)[None, :]
        mask = s_idx < t_idx
        log_fail = jnp.where(mask, jax.nn.log_sigmoid(-z), 0.0)
        later = lax.cumsum(log_fail, axis=3, reverse=True) - log_fail
        a = jnp.where(mask, jnp.exp(jax.nn.log_sigmoid(z) + later), 0.0)
        outs.append(jnp.einsum('bhts,bshd->bthd', a.astype(v.dtype), v[:, :kl]))
    return jnp.concatenate(outs, axis=1)


def causal_softmax_attention(q, k, v, scale):
    B, S, H, _ = q.shape
    outs = []
    for i in range(S // BLOCK):
        t0 = i * BLOCK
        kl = t0 + BLOCK
        s = jnp.einsum('bthd,bshd->bhts', q[:, t0:kl], k[:, :kl]).astype(jnp.float32) * scale
        mask = jnp.arange(kl)[None, :] <= (t0 + jnp.arange(BLOCK)[:, None])
        p = jax.nn.softmax(jnp.where(mask, s, -jnp.inf), axis=-1)
        outs.append(jnp.einsum('bhts,bshd->bthd', p.astype(v.dtype), v[:, :kl]))
    return jnp.concatenate(outs, axis=1)


def sliding_window_sink_attention(q, k, v, sinks):
    B, S, H, d = q.shape
    nb = S // BLOCK
    qb = q.reshape(B, nb, BLOCK, SWA_KV_HEADS, SWA_GROUP, d)

    def band(t):
        prev = jnp.pad(t, ((0, 0), (BLOCK, 0), (0, 0), (0, 0)))[:, :S]
        return jnp.concatenate([prev.reshape(B, nb, BLOCK, SWA_KV_HEADS, d),
                                t.reshape(B, nb, BLOCK, SWA_KV_HEADS, d)], axis=2)

    kb, vb = band(k), band(v)
    s = jnp.einsum('bnqkgd,bnskd->bnkgqs', qb, kb).astype(jnp.float32) / math.sqrt(d)
    rel = jnp.arange(BLOCK)[:, None] + BLOCK - jnp.arange(2 * BLOCK)[None, :]
    abs_s = jnp.arange(nb)[:, None] * BLOCK - BLOCK + jnp.arange(2 * BLOCK)[None, :]
    valid = ((rel >= 0) & (rel < SWA_WINDOW))[None, :, :] & (abs_s >= 0)[:, None, :]
    slopes = alibi_slopes(SWA_HEADS).reshape(SWA_KV_HEADS, SWA_GROUP)
    s = s - slopes[:, :, None, None] * rel.astype(jnp.float32)
    s = jnp.where(valid[None, :, None, None, :, :], s, -jnp.inf)
    sink = sinks.astype(jnp.float32).reshape(SWA_KV_HEADS, SWA_GROUP)[None, None, :, :, None, None]
    m = jnp.maximum(jnp.max(s, axis=-1, keepdims=True), sink)
    e = jnp.exp(s - m)
    p = e / (jnp.sum(e, axis=-1, keepdims=True) + jnp.exp(sink - m))
    o = jnp.einsum('bnkgqs,bnskd->bnqkgd', p.astype(v.dtype), vb)
    return o.reshape(B, S, H, d)


def stick_breaking_layer(x, norm_g, w_in, w_out):
    B, S, _ = x.shape
    proj = rms_norm(x, norm_g) @ w_in
    q, k, v, gate = jnp.split(proj, 4, axis=-1)
    q = q.reshape(B, S, SB_HEADS, SB_HEAD_DIM)
    k = k.reshape(B, S, SB_HEADS, SB_HEAD_DIM)
    v = v.reshape(B, S, SB_HEADS, SB_HEAD_DIM)
    o = stick_breaking_attention(q, k, v).reshape(B, S, SB_WIDTH)
    return x + (o * jax.nn.silu(gate)) @ w_out


def mla_layer(x, norm_g, w_in, q_a_norm, w_uq, kv_a_norm, w_ukv, q_head_norm, k_head_norm, w_out):
    B, S, _ = x.shape
    proj = rms_norm(x, norm_g) @ w_in
    c1 = MLA_Q_RANK
    c2 = c1 + MLA_KV_RANK
    c3 = c2 + MLA_ROPE_DIM
    q_lat, kv_lat, k_pe, gate = proj[..., :c1], proj[..., c1:c2], proj[..., c2:c3], proj[..., c3:]
    q = (rms_norm(q_lat, q_a_norm) @ w_uq).reshape(B, S, MLA_HEADS, MLA_QK_DIM)
    kv = (rms_norm(kv_lat, kv_a_norm) @ w_ukv).reshape(B, S, MLA_HEADS, MLA_NOPE_DIM + MLA_V_DIM)
    k_nope, v = kv[..., :MLA_NOPE_DIM], kv[..., MLA_NOPE_DIM:]
    k_pe = jnp.broadcast_to(k_pe[:, :, None, :], (B, S, MLA_HEADS, MLA_ROPE_DIM))
    k = jnp.concatenate([k_nope, k_pe], axis=-1)
    q = rms_norm(q, q_head_norm)
    k = rms_norm(k, k_head_norm)
    pos = jnp.arange(S)
    q = jnp.concatenate([q[..., :MLA_NOPE_DIM], rope(q[..., MLA_NOPE_DIM:], pos)], axis=-1)
    k = jnp.concatenate([k[..., :MLA_NOPE_DIM], rope(k[..., MLA_NOPE_DIM:], pos)], axis=-1)
    o = causal_softmax_attention(q, k, v, 1.0 / math.sqrt(MLA_QK_DIM)).reshape(B, S, MLA_WIDTH)
    return x + (o * jax.nn.silu(gate)) @ w_out


def swa_layer(x, norm_g, w_in, q_head_norm, k_head_norm, sinks, w_out):
    B, S, _ = x.shape
    proj = rms_norm(x, norm_g) @ w_in
    kv_w = SWA_KV_HEADS * SWA_HEAD_DIM
    c1 = SWA_WIDTH
    c2 = c1 + kv_w
    c3 = c2 + kv_w
    q = proj[..., :c1].reshape(B, S, SWA_HEADS, SWA_HEAD_DIM)
    k = proj[..., c1:c2].reshape(B, S, SWA_KV_HEADS, SWA_HEAD_DIM)
    v = proj[..., c2:c3].reshape(B, S, SWA_KV_HEADS, SWA_HEAD_DIM)
    gate = proj[..., c3:]
    q = rms_norm(q, q_head_norm)
    k = rms_norm(k, k_head_norm)
    o = sliding_window_sink_attention(q, k, v, sinks).reshape(B, S, SWA_WIDTH)
    return x + (o * jax.nn.silu(gate)) @ w_out


def _dense(key, fan_in, fan_out):
    return jax.random.normal(key, (fan_in, fan_out), jnp.float32) * fan_in ** -0.5


def _gain(key, n):
    return 1.0 + 0.02 * jax.random.normal(key, (n,), jnp.float32)


def setup_inputs(seed: int = 0) -> dict:
    key = jax.random.key(seed)
    ks = jax.random.split(key, 24)
    sb_in = 4 * SB_WIDTH
    mla_in = MLA_Q_RANK + MLA_KV_RANK + MLA_ROPE_DIM + MLA_WIDTH
    swa_in = SWA_WIDTH + 2 * SWA_KV_HEADS * SWA_HEAD_DIM + SWA_WIDTH
    return {
        'x': jax.random.normal(ks[0], (BATCH, SEQ, D_MODEL), jnp.float32),
        'l0_norm': _gain(ks[1], D_MODEL),
        'l0_w_in': _dense(ks[2], D_MODEL, sb_in),
        'l0_w_out': _dense(ks[3], SB_WIDTH, D_MODEL),
        'l1_norm': _gain(ks[4], D_MODEL),
        'l1_w_in': _dense(ks[5], D_MODEL, mla_in),
        'l1_q_a_norm': _gain(ks[6], MLA_Q_RANK),
        'l1_w_uq': _dense(ks[7], MLA_Q_RANK, MLA_HEADS * MLA_QK_DIM),
        'l1_kv_a_norm': _gain(ks[8], MLA_KV_RANK),
        'l1_w_ukv': _dense(ks[9], MLA_KV_RANK, MLA_HEADS * (MLA_NOPE_DIM + MLA_V_DIM)),
        'l1_q_head_norm': _gain(ks[10], MLA_QK_DIM),
        'l1_k_head_norm': _gain(ks[11], MLA_QK_DIM),
        'l1_w_out': _dense(ks[12], MLA_WIDTH, D_MODEL),
        'l2_norm': _gain(ks[13], D_MODEL),
        'l2_w_in': _dense(ks[14], D_MODEL, swa_in),
        'l2_q_head_norm': _gain(ks[15], SWA_HEAD_DIM),
        'l2_k_head_norm': _gain(ks[16], SWA_HEAD_DIM),
        'l2_sinks': 0.5 * jax.random.normal(ks[17], (SWA_HEADS,), jnp.float32),
        'l2_w_out': _dense(ks[18], SWA_WIDTH, D_MODEL),
        'l3_norm': _gain(ks[19], D_MODEL),
        'l3_w_in': _dense(ks[20], D_MODEL, sb_in),
        'l3_w_out': _dense(ks[21], SB_WIDTH, D_MODEL),
    }


def reference(x, l0_norm, l0_w_in, l0_w_out,
              l1_norm, l1_w_in, l1_q_a_norm, l1_w_uq, l1_kv_a_norm, l1_w_ukv,
              l1_q_head_norm, l1_k_head_norm, l1_w_out,
              l2_norm, l2_w_in, l2_q_head_norm, l2_k_head_norm, l2_sinks, l2_w_out,
              l3_norm, l3_w_in, l3_w_out):
    layer_params = [
        (l0_norm, l0_w_in, l0_w_out),
        (l1_norm, l1_w_in, l1_q_a_norm, l1_w_uq, l1_kv_a_norm, l1_w_ukv,
         l1_q_head_norm, l1_k_head_norm, l1_w_out),
        (l2_norm, l2_w_in, l2_q_head_norm, l2_k_head_norm, l2_sinks, l2_w_out),
        (l3_norm, l3_w_in, l3_w_out),
    ]
    mixers = (stick_breaking_layer, mla_layer, swa_layer)
    for i in range(DEPTH):
        x = mixers[i % N_MIXERS](x, *layer_params[i])
    return x
```

```python
import functools
import math

import jax
import jax.numpy as jnp
from jax import lax
from jax.experimental import pallas as pl
from jax.experimental.pallas import tpu as pltpu

NORM_EPS = 1e-6
LOG2E = 1.4426950408889634

LANES = 128
VMEM_LIMIT_BYTES = 56 * 1024 * 1024

SB_HEADS = 16
SB_HEAD_DIM = 64
SB_WIDTH = SB_HEADS * SB_HEAD_DIM

MLA_HEADS = 8
MLA_NOPE_DIM = 128
MLA_ROPE_DIM = 64
MLA_QK_DIM = MLA_NOPE_DIM + MLA_ROPE_DIM
MLA_QK_PAD = 256
MLA_V_DIM = 128
MLA_Q_RANK = 256
MLA_KV_RANK = 128
MLA_WIDTH = MLA_HEADS * MLA_V_DIM
ROPE_THETA = 10000.0

SWA_HEADS = 16
SWA_KV_HEADS = 4
SWA_GROUP = SWA_HEADS // SWA_KV_HEADS
SWA_HEAD_DIM = 64
SWA_WINDOW = 128
SWA_WIDTH = SWA_HEADS * SWA_HEAD_DIM

F32_EXP2_ZERO_BELOW = -150.0


def _cparams(*sem):
    return pltpu.CompilerParams(dimension_semantics=sem, vmem_limit_bytes=VMEM_LIMIT_BYTES)


def _rms_scale(xf, n):
    return lax.rsqrt(jnp.sum(xf * xf, axis=-1, keepdims=True) * (1.0 / n) + NORM_EPS)


def _norm_matmul_kernel(x_ref, g_ref, w_ref, o_ref, *, n_chunk):
    xf = x_ref[...]
    xn = (xf * _rms_scale(xf, xf.shape[-1]) * g_ref[...]).astype(jnp.bfloat16)
    n_out = o_ref.shape[-1]
    for c in range(0, n_out, n_chunk):
        o_ref[:, c:c + n_chunk] = jnp.dot(
            xn, w_ref[:, c:c + n_chunk], preferred_element_type=jnp.float32).astype(o_ref.dtype)


def norm_matmul(x2d, g, w_bf16, *, tm=512, n_chunk=512):
    n, d = x2d.shape
    n_out = w_bf16.shape[1]
    return pl.pallas_call(
        functools.partial(_norm_matmul_kernel, n_chunk=n_chunk),
        grid=(n // tm,),
        in_specs=[pl.BlockSpec((tm, d), lambda i: (i, 0)),
                  pl.BlockSpec((1, d), lambda i: (0, 0)),
                  pl.BlockSpec((d, n_out), lambda i: (0, 0))],
        out_specs=pl.BlockSpec((tm, n_out), lambda i: (i, 0)),
        out_shape=jax.ShapeDtypeStruct((n, n_out), jnp.bfloat16),
        compiler_params=_cparams("arbitrary"),
        name="norm_matmul",
    )(x2d, g.reshape(1, d), w_bf16)


def _gate_out_proj_kernel(o_ref, gate_ref, x_ref, w_ref, y_ref):
    gate = gate_ref[...].astype(jnp.float32)
    og = (o_ref[...].astype(jnp.float32) * (gate * jax.nn.sigmoid(gate))).astype(jnp.bfloat16)
    y_ref[...] = x_ref[...] + jnp.dot(og, w_ref[...], preferred_element_type=jnp.float32)


def gate_out_proj(o2d, proj2d, gate_block, x2d, w_bf16, *, tm=512):
    n, width = o2d.shape
    d = x2d.shape[1]
    return pl.pallas_call(
        _gate_out_proj_kernel,
        grid=(n // tm,),
        in_specs=[pl.BlockSpec((tm, width), lambda i: (i, 0)),
                  pl.BlockSpec((tm, width), lambda i: (i, gate_block)),
                  pl.BlockSpec((tm, d), lambda i: (i, 0)),
                  pl.BlockSpec((width, d), lambda i: (0, 0))],
        out_specs=pl.BlockSpec((tm, d), lambda i: (i, 0)),
        out_shape=jax.ShapeDtypeStruct((n, d), jnp.float32),
        compiler_params=_cparams("arbitrary"),
        name="gate_out_proj",
    )(o2d, proj2d, x2d, w_bf16)


def _sb_block(qm, k_blk, v_blk, tri, carry, mask):
    z2 = lax.dot_general(qm, k_blk, (((1,), (1,)), ((), ())),
                         preferred_element_type=jnp.float32) * LOG2E
    soft = jnp.log2(1.0 + jnp.exp2(-jnp.abs(z2)))
    ls2 = jnp.minimum(z2, 0.0) - soft
    lf2 = ls2 - z2
    if mask is not None:
        lf2 = jnp.where(mask, lf2, 0.0)
    hi = lf2.astype(jnp.bfloat16)
    lo = (lf2 - hi.astype(jnp.float32)).astype(jnp.bfloat16)
    later = (jnp.dot(hi, tri, preferred_element_type=jnp.float32)
             + jnp.dot(lo, tri, preferred_element_type=jnp.float32))
    a = jnp.exp2(ls2 + later + carry)
    if mask is not None:
        a = jnp.where(mask, a, 0.0)
    o_tile = jnp.dot(a.astype(jnp.bfloat16), v_blk, preferred_element_type=jnp.float32)
    return o_tile, carry + jnp.sum(lf2, axis=-1, keepdims=True)


def _sb_attn_kernel(q_ref, k_ref, v_ref, o_ref, *, tq, tk, head_dim):
    seq = q_ref.shape[1]
    lane = lax.broadcasted_iota(jnp.int32, (1, LANES), 1)
    row = lax.broadcasted_iota(jnp.int32, (tq, tk), 0)
    col = lax.broadcasted_iota(jnp.int32, (tq, tk), 1)
    tri = (lax.broadcasted_iota(jnp.int32, (tk, tk), 0)
           > lax.broadcasted_iota(jnp.int32, (tk, tk), 1)).astype(jnp.bfloat16)
    scale = 1.0 / math.sqrt(head_dim)
    diag_blocks = tq // tk

    def q_tile(i, _):
        t0 = pl.multiple_of(i * tq, tq)
        q = q_ref[0, pl.ds(t0, tq), :] * jnp.asarray(scale, q_ref.dtype)
        o_pair = None
        for hh in range(LANES // head_dim):
            in_head = (lane >= hh * head_dim) & (lane < (hh + 1) * head_dim)
            qm = jnp.where(in_head, q, jnp.zeros_like(q))
            acc = jnp.zeros((tq, LANES), jnp.float32)
            carry = jnp.zeros((tq, 1), jnp.float32)
            for d in range(diag_blocks - 1, -1, -1):
                s0 = pl.multiple_of(t0 + d * tk, tk)
                mask = (col + d * tk) < row
                o_t, carry = _sb_block(qm, k_ref[0, pl.ds(s0, tk), :], v_ref[0, pl.ds(s0, tk), :],
                                       tri, carry, mask)
                acc = acc + o_t

            def key_tile(jj, state):
                acc, carry = state
                s0 = pl.multiple_of(t0 - (jj + 1) * tk, tk)
                o_t, carry = _sb_block(qm, k_ref[0, pl.ds(s0, tk), :], v_ref[0, pl.ds(s0, tk), :],
                                       tri, carry, None)
                return acc + o_t, carry

            acc, carry = lax.fori_loop(0, i * diag_blocks, key_tile, (acc, carry))
            o_pair = acc if o_pair is None else jnp.where(in_head, acc, o_pair)
        o_ref[0, pl.ds(t0, tq), :] = o_pair.astype(o_ref.dtype)
        return 0

    lax.fori_loop(0, seq // tq, q_tile, 0)


def sb_attention(proj3d, *, tq=256, tk=256):
    b, s, _ = proj3d.shape
    cols = SB_WIDTH // LANES
    blk = lambda off: pl.BlockSpec((1, s, LANES), lambda bi, p: (bi, 0, off + p))
    return pl.pallas_call(
        functools.partial(_sb_attn_kernel, tq=tq, tk=tk, head_dim=SB_HEAD_DIM),
        grid=(b, cols),
        in_specs=[blk(0), blk(cols), blk(2 * cols)],
        out_specs=pl.BlockSpec((1, s, LANES), lambda bi, p: (bi, 0, p)),
        out_shape=jax.ShapeDtypeStruct((b, s, SB_WIDTH), jnp.bfloat16),
        compiler_params=_cparams("arbitrary", "arbitrary"),
        name="sb_attention",
    )(proj3d, proj3d, proj3d)


def stick_breaking_layer(x, norm_g, w_in, w_out):
    b, s, d = x.shape
    x2d = x.reshape(b * s, d)
    proj = norm_matmul(x2d, norm_g, w_in.astype(jnp.bfloat16))
    o = sb_attention(proj.reshape(b, s, -1))
    y = gate_out_proj(o.reshape(b * s, SB_WIDTH), proj, 3, x2d, w_out.astype(jnp.bfloat16))
    return y.reshape(b, s, d)


def _rope_tables(seq):
    half = MLA_ROPE_DIM // 2
    inv_freq = ROPE_THETA ** (-jnp.arange(half, dtype=jnp.float32) / half)
    ang = jnp.arange(seq, dtype=jnp.float32)[:, None] * inv_freq[None, :]
    cos, sin = jnp.cos(ang), jnp.sin(ang)
    zero = jnp.zeros((seq, LANES - MLA_ROPE_DIM), jnp.float32)
    return (jnp.concatenate([cos, cos, zero], axis=1),
            jnp.concatenate([-sin, sin, zero], axis=1))


def _rotate_half(pe):
    lane = lax.broadcasted_iota(jnp.int32, pe.shape, 1)
    half = MLA_ROPE_DIM // 2
    return jnp.where(lane < half, pltpu.roll(pe, LANES - half, axis=1), pltpu.roll(pe, half, axis=1))


def _mla_prep_kernel(qlat_ref, kvl_ref, cos_ref, sin_ref, qa_ref, kva_ref, gq_ref, gk_ref,
                     wuq_ref, wukv_ref, q_ref, k_ref, v_ref):
    cos, sin = cos_ref[...], sin_ref[...]
    ql = qlat_ref[...].astype(jnp.float32)
    qn = (ql * _rms_scale(ql, MLA_Q_RANK) * qa_ref[...]).astype(jnp.bfloat16)
    kvl = kvl_ref[...].astype(jnp.float32)
    kv_lat = kvl[:, :MLA_KV_RANK]
    kvn = (kv_lat * _rms_scale(kv_lat, MLA_KV_RANK) * kva_ref[...]).astype(jnp.bfloat16)
    gq, gk = gq_ref[...], gk_ref[...]
    k_pe = kvl[:, MLA_KV_RANK:]
    k_pe_sq = jnp.sum(k_pe * k_pe, axis=-1, keepdims=True)
    k_pe_g = k_pe * gk[:, MLA_NOPE_DIM:]
    k_pe_rot = k_pe_g * cos + _rotate_half(k_pe_g) * sin
    for h in range(MLA_HEADS):
        c0 = h * MLA_QK_PAD
        qh = jnp.dot(qn, wuq_ref[:, c0:c0 + MLA_QK_PAD], preferred_element_type=jnp.float32)
        qh = qh * _rms_scale(qh, MLA_QK_DIM) * gq
        q_pe = qh[:, MLA_NOPE_DIM:]
        q_ref[:, c0:c0 + MLA_NOPE_DIM] = qh[:, :MLA_NOPE_DIM].astype(q_ref.dtype)
        q_ref[:, c0 + MLA_NOPE_DIM:c0 + MLA_QK_PAD] = (
            q_pe * cos + _rotate_half(q_pe) * sin).astype(q_ref.dtype)
        kvh = jnp.dot(kvn, wukv_ref[:, c0:c0 + MLA_QK_PAD], preferred_element_type=jnp.float32)
        k_nope = kvh[:, :MLA_NOPE_DIM]
        r = lax.rsqrt((jnp.sum(k_nope * k_nope, axis=-1, keepdims=True) + k_pe_sq)
                      * (1.0 / MLA_QK_DIM) + NORM_EPS)
        k_ref[:, c0:c0 + MLA_NOPE_DIM] = (k_nope * r * gk[:, :MLA_NOPE_DIM]).astype(k_ref.dtype)
        k_ref[:, c0 + MLA_NOPE_DIM:c0 + MLA_QK_PAD] = (k_pe_rot * r).astype(k_ref.dtype)
        v_ref[:, h * MLA_V_DIM:(h + 1) * MLA_V_DIM] = kvh[:, MLA_NOPE_DIM:].astype(v_ref.dtype)


def mla_prep(proj2d, seq, q_a_norm, kv_a_norm, q_head_norm, k_head_norm, wuq_pad, wukv, *, tm=512):
    n = proj2d.shape[0]
    cos, sin = _rope_tables(seq)
    pad = jnp.zeros((MLA_QK_PAD - MLA_QK_DIM,), jnp.float32)
    gq = jnp.concatenate([q_head_norm, pad]).reshape(1, MLA_QK_PAD)
    gk = jnp.concatenate([k_head_norm, pad]).reshape(1, MLA_QK_PAD)
    qlat_blk = MLA_WIDTH // MLA_Q_RANK
    tiles_per_seq = seq // tm
    row = lambda i: (i, 0)
    const = lambda i: (0, 0)
    pos = lambda i: (i % tiles_per_seq, 0)
    return pl.pallas_call(
        _mla_prep_kernel,
        grid=(n // tm,),
        in_specs=[pl.BlockSpec((tm, MLA_Q_RANK), lambda i: (i, qlat_blk)),
                  pl.BlockSpec((tm, MLA_Q_RANK), lambda i: (i, qlat_blk + 1)),
                  pl.BlockSpec((tm, LANES), pos),
                  pl.BlockSpec((tm, LANES), pos),
                  pl.BlockSpec((1, MLA_Q_RANK), const),
                  pl.BlockSpec((1, MLA_KV_RANK), const),
                  pl.BlockSpec((1, MLA_QK_PAD), const),
                  pl.BlockSpec((1, MLA_QK_PAD), const),
                  pl.BlockSpec((MLA_Q_RANK, MLA_HEADS * MLA_QK_PAD), const),
                  pl.BlockSpec((MLA_KV_RANK, MLA_HEADS * MLA_QK_PAD), const)],
        out_specs=[pl.BlockSpec((tm, MLA_HEADS * MLA_QK_PAD), row),
                   pl.BlockSpec((tm, MLA_HEADS * MLA_QK_PAD), row),
                   pl.BlockSpec((tm, MLA_WIDTH), row)],
        out_shape=[jax.ShapeDtypeStruct((n, MLA_HEADS * MLA_QK_PAD), jnp.bfloat16),
                   jax.ShapeDtypeStruct((n, MLA_HEADS * MLA_QK_PAD), jnp.bfloat16),
                   jax.ShapeDtypeStruct((n, MLA_WIDTH), jnp.bfloat16)],
        compiler_params=_cparams("arbitrary"),
        name="mla_prep",
    )(proj2d, proj2d, cos, sin, q_a_norm.reshape(1, -1), kv_a_norm.reshape(1, -1), gq, gk,
      wuq_pad, wukv)


def _causal_attn_kernel(q_ref, k_ref, v_ref, o_ref, *, tq, tk, scale):
    seq = q_ref.shape[1]
    row = lax.broadcasted_iota(jnp.int32, (tq, tk), 0)
    col = lax.broadcasted_iota(jnp.int32, (tq, tk), 1)
    diag_blocks = tq // tk
    s_scale = scale * LOG2E

    def block(q, s0, state, mask):
        m, l, acc = state
        s2 = lax.dot_general(q, k_ref[0, pl.ds(s0, tk), :], (((1,), (1,)), ((), ())),
                             preferred_element_type=jnp.float32) * s_scale
        if mask is not None:
            s2 = jnp.where(mask, s2, -jnp.inf)
        m_new = jnp.maximum(m, jnp.max(s2, axis=-1, keepdims=True))
        p = jnp.exp2(s2 - m_new)
        alpha = jnp.exp2(m - m_new)
        l = alpha * l + jnp.sum(p, axis=-1, keepdims=True)
        acc = alpha * acc + jnp.dot(p.astype(jnp.bfloat16), v_ref[0, pl.ds(s0, tk), :],
                                    preferred_element_type=jnp.float32)
        return m_new, l, acc

    def q_tile(i, _):
        t0 = pl.multiple_of(i * tq, tq)
        q = q_ref[0, pl.ds(t0, tq), :]
        state = (jnp.full((tq, 1), -jnp.inf, jnp.float32), jnp.zeros((tq, 1), jnp.float32),
                 jnp.zeros((tq, v_ref.shape[-1]), jnp.float32))
        for d in range(diag_blocks):
            s0 = pl.multiple_of(t0 + d * tk, tk)
            state = block(q, s0, state, (col + d * tk) <= row)
        state = lax.fori_loop(
            0, i * diag_blocks,
            lambda j, st: block(q, pl.multiple_of(j * tk, tk), st, None), state)
        m, l, acc = state
        o_ref[0, pl.ds(t0, tq), :] = (acc / l).astype(o_ref.dtype)
        return 0

    lax.fori_loop(0, seq // tq, q_tile, 0)


def mla_attention(q3d, k3d, v3d, *, tq=256, tk=256):
    b, s, _ = q3d.shape
    qk = lambda bi, h: (bi, 0, h)
    return pl.pallas_call(
        functools.partial(_causal_attn_kernel, tq=tq, tk=tk, scale=1.0 / math.sqrt(MLA_QK_DIM)),
        grid=(b, MLA_HEADS),
        in_specs=[pl.BlockSpec((1, s, MLA_QK_PAD), qk),
                  pl.BlockSpec((1, s, MLA_QK_PAD), qk),
                  pl.BlockSpec((1, s, MLA_V_DIM), qk)],
        out_specs=pl.BlockSpec((1, s, MLA_V_DIM), qk),
        out_shape=jax.ShapeDtypeStruct((b, s, MLA_WIDTH), jnp.bfloat16),
        compiler_params=_cparams("arbitrary", "arbitrary"),
        name="mla_attention",
    )(q3d, k3d, v3d)


def mla_layer(x, norm_g, w_in, q_a_norm, w_uq, kv_a_norm, w_ukv, q_head_norm, k_head_norm, w_out):
    b, s, d = x.shape
    x2d = x.reshape(b * s, d)
    c1 = MLA_Q_RANK
    c3 = c1 + MLA_KV_RANK + MLA_ROPE_DIM
    w_in_l = jnp.concatenate(
        [w_in[:, c3:], w_in[:, :c3], jnp.zeros((d, LANES - MLA_ROPE_DIM), w_in.dtype)], axis=1)
    wuq_pad = jnp.pad(w_uq.reshape(MLA_Q_RANK, MLA_HEADS, MLA_QK_DIM),
                      ((0, 0), (0, 0), (0, MLA_QK_PAD - MLA_QK_DIM))).reshape(MLA_Q_RANK, -1)
    proj = norm_matmul(x2d, norm_g, w_in_l.astype(jnp.bfloat16))
    q, k, v = mla_prep(proj, s, q_a_norm, kv_a_norm, q_head_norm, k_head_norm,
                       wuq_pad.astype(jnp.bfloat16), w_ukv.astype(jnp.bfloat16))
    o = mla_attention(q.reshape(b, s, -1), k.reshape(b, s, -1), v.reshape(b, s, -1))
    y = gate_out_proj(o.reshape(b * s, MLA_WIDTH), proj, 0, x2d, w_out.astype(jnp.bfloat16))
    return y.reshape(b, s, d)


def _halves_rms(xf, gain):
    lane = lax.broadcasted_iota(jnp.int32, xf.shape, 1)
    lo = lane < SWA_HEAD_DIM
    sq = xf * xf
    s_lo = jnp.sum(jnp.where(lo, sq, 0.0), axis=-1, keepdims=True)
    s_hi = jnp.sum(jnp.where(lo, 0.0, sq), axis=-1, keepdims=True)
    r = lax.rsqrt(jnp.where(lo, s_lo, s_hi) * (1.0 / SWA_HEAD_DIM) + NORM_EPS)
    return xf * r * gain


def _swa_attn_kernel(q_ref, k_ref, v_ref, gq_ref, gk_ref, slope_ref, sink_ref, o_ref, kn_ref, *, tq):
    seq = q_ref.shape[1]
    ncol = q_ref.shape[2] // LANES
    tk = tq + SWA_WINDOW
    pair = pl.program_id(1)
    kn_ref[pl.ds(0, SWA_WINDOW), :] = jnp.zeros((SWA_WINDOW, LANES), kn_ref.dtype)
    kn_ref[pl.ds(SWA_WINDOW, seq), :] = _halves_rms(
        k_ref[0].astype(jnp.float32), gk_ref[...]).astype(kn_ref.dtype)
    lane = lax.broadcasted_iota(jnp.int32, (1, LANES), 1)
    rel = (lax.broadcasted_iota(jnp.int32, (tq, tk), 0) + SWA_WINDOW
           - lax.broadcasted_iota(jnp.int32, (tq, tk), 1))
    in_window = (rel >= 0) & (rel < SWA_WINDOW)
    rel_f = rel.astype(jnp.float32)
    col_idx = lax.broadcasted_iota(jnp.int32, (tq, tk), 1)
    inv_sqrt_d = 1.0 / math.sqrt(SWA_HEAD_DIM)

    def q_tile(i, _):
        t0 = pl.multiple_of(i * tq, tq)
        kb = kn_ref[pl.ds(t0, tk), :]
        valid = in_window & (col_idx + t0 >= SWA_WINDOW)
        prev0 = pl.multiple_of(jnp.maximum(t0 - SWA_WINDOW, 0), SWA_WINDOW)
        v_prev = v_ref[0, pl.ds(prev0, SWA_WINDOW), :]
        vb = jnp.concatenate([v_prev, v_ref[0, pl.ds(t0, tq), :]], axis=0)
        for c in range(ncol):
            qn = _halves_rms(q_ref[0, pl.ds(t0, tq), c * LANES:(c + 1) * LANES].astype(jnp.float32),
                             gq_ref[...]).astype(jnp.bfloat16)
            o_col = None
            for half in range(2):
                in_half = (lane >= half * SWA_HEAD_DIM) & (lane < (half + 1) * SWA_HEAD_DIM)
                qm = jnp.where(in_half, qn, jnp.zeros_like(qn))
                s = lax.dot_general(qm, kb, (((1,), (1,)), ((), ())),
                                    preferred_element_type=jnp.float32) * inv_sqrt_d
                s = s - slope_ref[pair, half, c] * rel_f
                s = jnp.where(valid, s, -jnp.inf)
                sink = sink_ref[pair, half, c]
                m = jnp.maximum(jnp.max(s, axis=-1, keepdims=True), sink)
                e = jnp.exp(s - m)
                p = e / (jnp.sum(e, axis=-1, keepdims=True) + jnp.exp(sink - m))
                o_h = jnp.dot(p.astype(jnp.bfloat16), vb, preferred_element_type=jnp.float32)
                o_col = o_h if o_col is None else jnp.where(in_half, o_h, o_col)
            o_ref[0, pl.ds(t0, tq), c * LANES:(c + 1) * LANES] = o_col.astype(o_ref.dtype)
        return 0

    lax.fori_loop(0, seq // tq, q_tile, 0)


def swa_attention(proj3d, gq2, gk2, slopes, sinks, *, tq=128):
    b, s, _ = proj3d.shape
    npair = SWA_KV_HEADS // 2
    qw = SWA_WIDTH // npair
    kcol0 = 2 * SWA_WIDTH // LANES
    smem = pl.BlockSpec(memory_space=pltpu.SMEM)
    return pl.pallas_call(
        functools.partial(_swa_attn_kernel, tq=tq),
        grid=(b, npair),
        in_specs=[pl.BlockSpec((1, s, qw), lambda bi, p: (bi, 0, p)),
                  pl.BlockSpec((1, s, LANES), lambda bi, p: (bi, 0, kcol0 + p)),
                  pl.BlockSpec((1, s, LANES), lambda bi, p: (bi, 0, kcol0 + npair + p)),
                  pl.BlockSpec((1, LANES), lambda bi, p: (0, 0)),
                  pl.BlockSpec((1, LANES), lambda bi, p: (0, 0)),
                  smem, smem],
        out_specs=pl.BlockSpec((1, s, qw), lambda bi, p: (bi, 0, p)),
        out_shape=jax.ShapeDtypeStruct((b, s, SWA_WIDTH), jnp.bfloat16),
        scratch_shapes=[pltpu.VMEM((s + SWA_WINDOW, LANES), jnp.bfloat16)],
        compiler_params=_cparams("arbitrary", "arbitrary"),
        name="swa_attention",
    )(proj3d, proj3d, proj3d, gq2, gk2, slopes, sinks)


def _swa_head_order():
    order = []
    for p in range(SWA_KV_HEADS // 2):
        for c in range(SWA_GROUP):
            order += [2 * p * SWA_GROUP + c, (2 * p + 1) * SWA_GROUP + c]
    return order


def swa_layer(x, norm_g, w_in, q_head_norm, k_head_norm, sinks, w_out):
    b, s, d = x.shape
    x2d = x.reshape(b * s, d)
    kv_w = SWA_KV_HEADS * SWA_HEAD_DIM
    c1, c2, c3 = SWA_WIDTH, SWA_WIDTH + kv_w, SWA_WIDTH + 2 * kv_w
    order = jnp.asarray(_swa_head_order())
    perm = (order[:, None] * SWA_HEAD_DIM + jnp.arange(SWA_HEAD_DIM)[None, :]).reshape(-1)
    w_in_l = jnp.concatenate([w_in[:, :c1][:, perm], w_in[:, c3:][:, perm], w_in[:, c1:c3]], axis=1)
    w_out_l = w_out[perm, :]
    slopes = 2.0 ** (-8.0 * jnp.arange(1, SWA_HEADS + 1, dtype=jnp.float32) / SWA_HEADS)
    by_pos = lambda t: t[order].reshape(SWA_KV_HEADS // 2, SWA_GROUP, 2).transpose(0, 2, 1)
    gq2 = jnp.tile(q_head_norm, 2).reshape(1, LANES)
    gk2 = jnp.tile(k_head_norm, 2).reshape(1, LANES)
    proj = norm_matmul(x2d, norm_g, w_in_l.astype(jnp.bfloat16))
    o = swa_attention(proj.reshape(b, s, -1), gq2, gk2, by_pos(slopes),
                      by_pos(sinks.astype(jnp.float32)))
    y = gate_out_proj(o.reshape(b * s, SWA_WIDTH), proj, 1, x2d, w_out_l.astype(jnp.bfloat16))
    return y.reshape(b, s, d)


def kernel(x, l0_norm, l0_w_in, l0_w_out, l1_norm, l1_w_in, l1_q_a_norm, l1_w_uq, l1_kv_a_norm,
           l1_w_ukv, l1_q_head_norm, l1_k_head_norm, l1_w_out, l2_norm, l2_w_in, l2_q_head_norm,
           l2_k_head_norm, l2_sinks, l2_w_out, l3_norm, l3_w_in, l3_w_out):
    x = stick_breaking_layer(x, l0_norm, l0_w_in, l0_w_out)
    x = mla_layer(x, l1_norm, l1_w_in, l1_q_a_norm, l1_w_uq, l1_kv_a_norm, l1_w_ukv,
                  l1_q_head_norm, l1_k_head_norm, l1_w_out)
    x = swa_layer(x, l2_norm, l2_w_in, l2_q_head_norm, l2_k_head_norm, l2_sinks, l2_w_out)
    x = stick_breaking_layer(x, l3_norm, l3_w_in, l3_w_out)
    return x
```

```python
import functools
import math

import jax
import jax.numpy as jnp
from jax import lax
from jax.experimental import pallas as pl
from jax.experimental.pallas import tpu as pltpu

NORM_EPS = 1e-6
LOG2E = 1.4426950408889634

LANES = 128
VMEM_LIMIT_BYTES = 56 * 1024 * 1024

SB_HEADS = 16
SB_HEAD_DIM = 64
SB_WIDTH = SB_HEADS * SB_HEAD_DIM

MLA_HEADS = 8
MLA_NOPE_DIM = 128
MLA_ROPE_DIM = 64
MLA_QK_DIM = MLA_NOPE_DIM + MLA_ROPE_DIM
MLA_QK_PAD = 256
MLA_V_DIM = 128
MLA_Q_RANK = 256
MLA_KV_RANK = 128
MLA_WIDTH = MLA_HEADS * MLA_V_DIM
ROPE_THETA = 10000.0

SWA_HEADS = 16
SWA_KV_HEADS = 4
SWA_GROUP = SWA_HEADS // SWA_KV_HEADS
SWA_HEAD_DIM = 64
SWA_WINDOW = 128
SWA_WIDTH = SWA_HEADS * SWA_HEAD_DIM

F32_EXP2_ZERO_BELOW = -160.0


def _cparams(*sem):
    return pltpu.CompilerParams(dimension_semantics=sem, vmem_limit_bytes=VMEM_LIMIT_BYTES)


def _rms_scale(xf, n):
    return lax.rsqrt(jnp.sum(xf * xf, axis=-1, keepdims=True) * (1.0 / n) + NORM_EPS)


def _norm_matmul_kernel(x_ref, g_ref, w_ref, o_ref, *, n_chunk):
    xf = x_ref[...]
    xn = (xf * _rms_scale(xf, xf.shape[-1]) * g_ref[...]).astype(jnp.bfloat16)
    n_out = o_ref.shape[-1]
    for c in range(0, n_out, n_chunk):
        o_ref[:, c:c + n_chunk] = jnp.dot(
            xn, w_ref[:, c:c + n_chunk], preferred_element_type=jnp.float32).astype(o_ref.dtype)


def norm_matmul(x2d, g, w_bf16, *, tm=512, n_chunk=512):
    n, d = x2d.shape
    n_out = w_bf16.shape[1]
    return pl.pallas_call(
        functools.partial(_norm_matmul_kernel, n_chunk=n_chunk),
        grid=(n // tm,),
        in_specs=[pl.BlockSpec((tm, d), lambda i: (i, 0)),
                  pl.BlockSpec((1, d), lambda i: (0, 0)),
                  pl.BlockSpec((d, n_out), lambda i: (0, 0))],
        out_specs=pl.BlockSpec((tm, n_out), lambda i: (i, 0)),
        out_shape=jax.ShapeDtypeStruct((n, n_out), jnp.bfloat16),
        compiler_params=_cparams("arbitrary"),
        name="norm_matmul",
    )(x2d, g.reshape(1, d), w_bf16)


def _gate_out_proj_kernel(o_ref, gate_ref, x_ref, w_ref, y_ref):
    gate = gate_ref[...].astype(jnp.float32)
    og = (o_ref[...].astype(jnp.float32) * (gate * jax.nn.sigmoid(gate))).astype(jnp.bfloat16)
    y_ref[...] = x_ref[...] + jnp.dot(og, w_ref[...], preferred_element_type=jnp.float32)


def gate_out_proj(o2d, proj2d, gate_block, x2d, w_bf16, *, tm=512):
    n, width = o2d.shape
    d = x2d.shape[1]
    return pl.pallas_call(
        _gate_out_proj_kernel,
        grid=(n // tm,),
        in_specs=[pl.BlockSpec((tm, width), lambda i: (i, 0)),
                  pl.BlockSpec((tm, width), lambda i: (i, gate_block)),
                  pl.BlockSpec((tm, d), lambda i: (i, 0)),
                  pl.BlockSpec((width, d), lambda i: (0, 0))],
        out_specs=pl.BlockSpec((tm, d), lambda i: (i, 0)),
        out_shape=jax.ShapeDtypeStruct((n, d), jnp.float32),
        compiler_params=_cparams("arbitrary"),
        name="gate_out_proj",
    )(o2d, proj2d, x2d, w_bf16)


def _sb_tiles(tiles, qm, acc, carry, tri2):
    heads = range(len(qm))
    z = [[lax.dot_general(k_blk, qm[h], (((1,), (1,)), ((), ())),
                          preferred_element_type=jnp.float32) for h in heads]
         for k_blk, _, _ in tiles]
    mid = []
    for (_, _, mask), z_tile in zip(tiles, z):
        row = []
        for h in heads:
            z2 = z_tile[h] * LOG2E
            neg_abs = lax.bitcast_convert_type(
                lax.bitcast_convert_type(z2, jnp.uint32) | jnp.uint32(0x80000000), jnp.float32)
            soft = jnp.log2(1.0 + jnp.exp2(neg_abs))
            ls2 = jnp.minimum(z2, 0.0) - soft
            lf2 = ls2 - z2
            if mask is not None:
                lf2 = jnp.where(mask, lf2, 0.0)
            hi = lf2.astype(jnp.bfloat16)
            lo = (lf2 - hi.astype(jnp.float32)).astype(jnp.bfloat16)
            later = jnp.dot(tri2, jnp.concatenate([hi, lo], axis=0),
                            preferred_element_type=jnp.float32)
            row.append((ls2, later, jnp.sum(lf2, axis=0, keepdims=True)))
        mid.append(row)
    acc, carry = list(acc), list(carry)
    for (_, vt_blks, mask), row in zip(tiles, mid):
        for h in heads:
            ls2, later, colsum = row[h]
            a = jnp.exp2(ls2 + later + carry[h])
            if mask is not None:
                a = jnp.where(mask, a, 0.0)
            acc[h] = acc[h] + jnp.dot(vt_blks[h], a.astype(jnp.bfloat16),
                                      preferred_element_type=jnp.float32)
            carry[h] = carry[h] + colsum
    return acc, carry


def _transpose_rows_into(src_ref, dst_ref, chunk):
    def body(c, _):
        r0 = pl.multiple_of(c * chunk, chunk)
        dst_ref[:, pl.ds(r0, chunk)] = (
            src_ref[0, pl.ds(r0, chunk), :].astype(jnp.float32).T.astype(dst_ref.dtype))
        return 0
    lax.fori_loop(0, src_ref.shape[1] // chunk, body, 0)


def _sb_attn_kernel(q_ref, k_ref, v_ref, o_ref, vt_ref, *, tile, head_dim):
    seq = q_ref.shape[1]
    n_heads = LANES // head_dim
    lane = lax.broadcasted_iota(jnp.int32, (1, LANES), 1)
    in_head = [(lane >= hh * head_dim) & (lane < (hh + 1) * head_dim) for hh in range(n_heads)]
    earlier = (lax.broadcasted_iota(jnp.int32, (tile, tile), 0)
               < lax.broadcasted_iota(jnp.int32, (tile, tile), 1))
    tri = earlier.astype(jnp.bfloat16)
    tri2 = jnp.concatenate([tri, tri], axis=1)
    scale = 1.0 / math.sqrt(head_dim)
    _transpose_rows_into(v_ref, vt_ref, tile)

    def q_tile(i, with_prev):
        t0 = pl.multiple_of(i * tile, tile)
        q = q_ref[0, pl.ds(t0, tile), :] * jnp.asarray(scale, q_ref.dtype)
        qm = [jnp.where(m, q, jnp.zeros_like(q)) for m in in_head]
        acc = [jnp.zeros((head_dim, tile), jnp.float32) for _ in range(n_heads)]
        carry = [jnp.zeros((1, tile), jnp.float32) for _ in range(n_heads)]

        def key_tile(s0, mask):
            s0 = pl.multiple_of(s0, tile)
            return (k_ref[0, pl.ds(s0, tile), :],
                    [vt_ref[hh * head_dim:(hh + 1) * head_dim, pl.ds(s0, tile)]
                     for hh in range(n_heads)], mask)

        def alive(carry):
            return jnp.max(functools.reduce(jnp.maximum, carry)) >= F32_EXP2_ZERO_BELOW

        tiles = [key_tile(t0, earlier)] + ([key_tile(t0 - tile, None)] if with_prev else [])
        acc, carry = _sb_tiles(tiles, qm, acc, carry, tri2)
        if with_prev:
            def cond(state):
                jj, live, _, _ = state
                return (jj <= i) & live

            def body(state):
                jj, _, acc, carry = state
                acc, carry = _sb_tiles([key_tile(t0 - jj * tile, None)], qm, acc, carry, tri2)
                return jj + 1, alive(carry), tuple(acc), tuple(carry)

            _, _, acc, carry = lax.while_loop(
                cond, body, (jnp.int32(2), alive(carry), tuple(acc), tuple(carry)))
        o_ref[0, pl.ds(t0, tile), :] = jnp.concatenate(acc, axis=0).T.astype(o_ref.dtype)

    q_tile(0, False)

    def q_tile_loop(i, _):
        q_tile(i, True)
        return 0

    lax.fori_loop(1, seq // tile, q_tile_loop, 0)


def sb_attention(proj3d, *, tile=256):
    b, s, _ = proj3d.shape
    cols = SB_WIDTH // LANES
    blk = lambda off: pl.BlockSpec((1, s, LANES), lambda bi, p: (bi, 0, off + p))
    return pl.pallas_call(
        functools.partial(_sb_attn_kernel, tile=tile, head_dim=SB_HEAD_DIM),
        grid=(b, cols),
        in_specs=[blk(0), blk(cols), blk(2 * cols)],
        out_specs=pl.BlockSpec((1, s, LANES), lambda bi, p: (bi, 0, p)),
        out_shape=jax.ShapeDtypeStruct((b, s, SB_WIDTH), jnp.bfloat16),
        scratch_shapes=[pltpu.VMEM((LANES, s), jnp.bfloat16)],
        compiler_params=_cparams("arbitrary", "arbitrary"),
        name="sb_attention",
    )(proj3d, proj3d, proj3d)


def stick_breaking_layer(x, norm_g, w_in, w_out):
    b, s, d = x.shape
    x2d = x.reshape(b * s, d)
    proj = norm_matmul(x2d, norm_g, w_in.astype(jnp.bfloat16))
    o = sb_attention(proj.reshape(b, s, -1))
    y = gate_out_proj(o.reshape(b * s, SB_WIDTH), proj, 3, x2d, w_out.astype(jnp.bfloat16))
    return y.reshape(b, s, d)


def _rope_tables(seq):
    half = MLA_ROPE_DIM // 2
    inv_freq = ROPE_THETA ** (-jnp.arange(half, dtype=jnp.float32) / half)
    ang = jnp.arange(seq, dtype=jnp.float32)[:, None] * inv_freq[None, :]
    cos, sin = jnp.cos(ang), jnp.sin(ang)
    zero = jnp.zeros((seq, LANES - MLA_ROPE_DIM), jnp.float32)
    return (jnp.concatenate([cos, cos, zero], axis=1),
            jnp.concatenate([-sin, sin, zero], axis=1))


def _rotate_half(pe):
    lane = lax.broadcasted_iota(jnp.int32, pe.shape, 1)
    half = MLA_ROPE_DIM // 2
    return jnp.where(lane < half, pltpu.roll(pe, LANES - half, axis=1), pltpu.roll(pe, half, axis=1))


def _mla_prep_kernel(qlat_ref, kvl_ref, cos_ref, sin_ref, qa_ref, kva_ref, gq_ref, gk_ref,
                     wuq_ref, wukv_ref, q_ref, k_ref, v_ref):
    cos, sin = cos_ref[...], sin_ref[...]
    ql = qlat_ref[...].astype(jnp.float32)
    qn = (ql * _rms_scale(ql, MLA_Q_RANK) * qa_ref[...]).astype(jnp.bfloat16)
    kvl = kvl_ref[...].astype(jnp.float32)
    kv_lat = kvl[:, :MLA_KV_RANK]
    kvn = (kv_lat * _rms_scale(kv_lat, MLA_KV_RANK) * kva_ref[...]).astype(jnp.bfloat16)
    gq, gk = gq_ref[...], gk_ref[...]
    k_pe = kvl[:, MLA_KV_RANK:]
    k_pe_sq = jnp.sum(k_pe * k_pe, axis=-1, keepdims=True)
    k_pe_g = k_pe * gk[:, MLA_NOPE_DIM:]
    k_pe_rot = k_pe_g * cos + _rotate_half(k_pe_g) * sin
    for h in range(MLA_HEADS):
        c0 = h * MLA_QK_PAD
        qh = jnp.dot(qn, wuq_ref[:, c0:c0 + MLA_QK_PAD], preferred_element_type=jnp.float32)
        qh = qh * _rms_scale(qh, MLA_QK_DIM) * gq
        q_pe = qh[:, MLA_NOPE_DIM:]
        q_ref[:, c0:c0 + MLA_NOPE_DIM] = qh[:, :MLA_NOPE_DIM].astype(q_ref.dtype)
        q_ref[:, c0 + MLA_NOPE_DIM:c0 + MLA_QK_PAD] = (
            q_pe * cos + _rotate_half(q_pe) * sin).astype(q_ref.dtype)
        kvh = jnp.dot(kvn, wukv_ref[:, c0:c0 + MLA_QK_PAD], preferred_element_type=jnp.float32)
        k_nope = kvh[:, :MLA_NOPE_DIM]
        r = lax.rsqrt((jnp.sum(k_nope * k_nope, axis=-1, keepdims=True) + k_pe_sq)
                      * (1.0 / MLA_QK_DIM) + NORM_EPS)
        k_ref[:, c0:c0 + MLA_NOPE_DIM] = (k_nope * r * gk[:, :MLA_NOPE_DIM]).astype(k_ref.dtype)
        k_ref[:, c0 + MLA_NOPE_DIM:c0 + MLA_QK_PAD] = (k_pe_rot * r).astype(k_ref.dtype)
        v_ref[:, h * MLA_V_DIM:(h + 1) * MLA_V_DIM] = kvh[:, MLA_NOPE_DIM:].astype(v_ref.dtype)


def mla_prep(proj2d, seq, q_a_norm, kv_a_norm, q_head_norm, k_head_norm, wuq_pad, wukv, *, tm=512):
    n = proj2d.shape[0]
    cos, sin = _rope_tables(seq)
    pad = jnp.zeros((MLA_QK_PAD - MLA_QK_DIM,), jnp.float32)
    gq = jnp.concatenate([q_head_norm, pad]).reshape(1, MLA_QK_PAD)
    gk = jnp.concatenate([k_head_norm, pad]).reshape(1, MLA_QK_PAD)
    qlat_blk = MLA_WIDTH // MLA_Q_RANK
    tiles_per_seq = seq // tm
    row = lambda i: (i, 0)
    const = lambda i: (0, 0)
    pos = lambda i: (i % tiles_per_seq, 0)
    return pl.pallas_call(
        _mla_prep_kernel,
        grid=(n // tm,),
        in_specs=[pl.BlockSpec((tm, MLA_Q_RANK), lambda i: (i, qlat_blk)),
                  pl.BlockSpec((tm, MLA_Q_RANK), lambda i: (i, qlat_blk + 1)),
                  pl.BlockSpec((tm, LANES), pos),
                  pl.BlockSpec((tm, LANES), pos),
                  pl.BlockSpec((1, MLA_Q_RANK), const),
                  pl.BlockSpec((1, MLA_KV_RANK), const),
                  pl.BlockSpec((1, MLA_QK_PAD), const),
                  pl.BlockSpec((1, MLA_QK_PAD), const),
                  pl.BlockSpec((MLA_Q_RANK, MLA_HEADS * MLA_QK_PAD), const),
                  pl.BlockSpec((MLA_KV_RANK, MLA_HEADS * MLA_QK_PAD), const)],
        out_specs=[pl.BlockSpec((tm, MLA_HEADS * MLA_QK_PAD), row),
                   pl.BlockSpec((tm, MLA_HEADS * MLA_QK_PAD), row),
                   pl.BlockSpec((tm, MLA_WIDTH), row)],
        out_shape=[jax.ShapeDtypeStruct((n, MLA_HEADS * MLA_QK_PAD), jnp.bfloat16),
                   jax.ShapeDtypeStruct((n, MLA_HEADS * MLA_QK_PAD), jnp.bfloat16),
                   jax.ShapeDtypeStruct((n, MLA_WIDTH), jnp.bfloat16)],
        compiler_params=_cparams("arbitrary"),
        name="mla_prep",
    )(proj2d, proj2d, cos, sin, q_a_norm.reshape(1, -1), kv_a_norm.reshape(1, -1), gq, gk,
      wuq_pad, wukv)


def _causal_attn_kernel(q_ref, k_ref, v_ref, o_ref, vt_ref, *, tile, n_chains, scale):
    seq = q_ref.shape[1]
    s_scale = scale * LOG2E
    group = n_chains * tile
    _transpose_rows_into(v_ref, vt_ref, tile)

    def blocks(work, states):
        s = [lax.dot_general(k_blk, q, (((1,), (1,)), ((), ())), preferred_element_type=jnp.float32)
             for q, k_blk, _, _ in work]
        mid = []
        for (_, _, vt_blk, mask), s_c, (m, l, acc) in zip(work, s, states):
            s2 = s_c * s_scale
            if mask is not None:
                s2 = jnp.where(mask, s2, -jnp.inf)
            m_new = jnp.maximum(m, jnp.max(s2, axis=0, keepdims=True))
            p = jnp.exp2(s2 - m_new)
            alpha = jnp.exp2(m - m_new)
            l = alpha * l + jnp.sum(p, axis=0, keepdims=True)
            pv = jnp.dot(vt_blk, p.astype(jnp.bfloat16), preferred_element_type=jnp.float32)
            mid.append((m_new, l, alpha, pv))
        return tuple((m_new, l, alpha * acc + pv)
                     for (m_new, l, alpha, pv), (_, _, acc) in zip(mid, states))

    def kv_tile(s0, rows):
        s0 = pl.multiple_of(s0, tile)
        return k_ref[0, pl.ds(s0, rows), :], vt_ref[:, pl.ds(s0, rows)]

    def q_group(i, _):
        t0 = pl.multiple_of(i * group, group)
        q = [q_ref[0, pl.ds(t0 + c * tile, tile), :] for c in range(n_chains)]
        init = (jnp.full((1, tile), -jnp.inf, jnp.float32), jnp.zeros((1, tile), jnp.float32),
                jnp.zeros((v_ref.shape[-1], tile), jnp.float32))
        work = []
        for c in range(n_chains):
            rows = (c + 1) * tile
            causal = (lax.broadcasted_iota(jnp.int32, (rows, tile), 0)
                      <= lax.broadcasted_iota(jnp.int32, (rows, tile), 1) + c * tile)
            work.append((q[c],) + kv_tile(t0, rows) + (causal,))
        states = blocks(work, [init] * n_chains)

        def full_tile(j, states):
            k_blk, vt_blk = kv_tile(j * group, group)
            return blocks([(q[c], k_blk, vt_blk, None) for c in range(n_chains)], states)

        states = lax.fori_loop(0, i, full_tile, states)
        for c in range(n_chains):
            m, l, acc = states[c]
            o_ref[0, pl.ds(t0 + c * tile, tile), :] = (acc / l).T.astype(o_ref.dtype)
        return 0

    lax.fori_loop(0, seq // group, q_group, 0)


def mla_attention(q3d, k3d, v3d, *, tile=256, n_chains=2):
    b, s, _ = q3d.shape
    qk = lambda bi, h: (bi, 0, h)
    return pl.pallas_call(
        functools.partial(_causal_attn_kernel, tile=tile, n_chains=n_chains,
                          scale=1.0 / math.sqrt(MLA_QK_DIM)),
        grid=(b, MLA_HEADS),
        in_specs=[pl.BlockSpec((1, s, MLA_QK_PAD), qk),
                  pl.BlockSpec((1, s, MLA_QK_PAD), qk),
                  pl.BlockSpec((1, s, MLA_V_DIM), qk)],
        out_specs=pl.BlockSpec((1, s, MLA_V_DIM), qk),
        out_shape=jax.ShapeDtypeStruct((b, s, MLA_WIDTH), jnp.bfloat16),
        scratch_shapes=[pltpu.VMEM((MLA_V_DIM, s), jnp.bfloat16)],
        compiler_params=_cparams("arbitrary", "arbitrary"),
        name="mla_attention",
    )(q3d, k3d, v3d)


def mla_layer(x, norm_g, w_in, q_a_norm, w_uq, kv_a_norm, w_ukv, q_head_norm, k_head_norm, w_out):
    b, s, d = x.shape
    x2d = x.reshape(b * s, d)
    c1 = MLA_Q_RANK
    c3 = c1 + MLA_KV_RANK + MLA_ROPE_DIM
    w_in_l = jnp.concatenate(
        [w_in[:, c3:], w_in[:, :c3], jnp.zeros((d, LANES - MLA_ROPE_DIM), w_in.dtype)], axis=1)
    wuq_pad = jnp.pad(w_uq.reshape(MLA_Q_RANK, MLA_HEADS, MLA_QK_DIM),
                      ((0, 0), (0, 0), (0, MLA_QK_PAD - MLA_QK_DIM))).reshape(MLA_Q_RANK, -1)
    proj = norm_matmul(x2d, norm_g, w_in_l.astype(jnp.bfloat16))
    q, k, v = mla_prep(proj, s, q_a_norm, kv_a_norm, q_head_norm, k_head_norm,
                       wuq_pad.astype(jnp.bfloat16), w_ukv.astype(jnp.bfloat16))
    o = mla_attention(q.reshape(b, s, -1), k.reshape(b, s, -1), v.reshape(b, s, -1))
    y = gate_out_proj(o.reshape(b * s, MLA_WIDTH), proj, 0, x2d, w_out.astype(jnp.bfloat16))
    return y.reshape(b, s, d)


def _halves_rms(xf, gain):
    lane = lax.broadcasted_iota(jnp.int32, xf.shape, 1)
    lo = lane < SWA_HEAD_DIM
    sq = xf * xf
    s_lo = jnp.sum(jnp.where(lo, sq, 0.0), axis=-1, keepdims=True)
    s_hi = jnp.sum(jnp.where(lo, 0.0, sq), axis=-1, keepdims=True)
    r = lax.rsqrt(jnp.where(lo, s_lo, s_hi) * (1.0 / SWA_HEAD_DIM) + NORM_EPS)
    return xf * r * gain


def _swa_attn_kernel(q_ref, k_ref, v_ref, gq_ref, gk_ref, slope_ref, sink_ref, o_ref, kn_ref, *, tq):
    seq = q_ref.shape[1]
    ncol = q_ref.shape[2] // LANES
    tk = tq + SWA_WINDOW
    pair = pl.program_id(1)
    kn_ref[pl.ds(0, SWA_WINDOW), :] = jnp.zeros((SWA_WINDOW, LANES), kn_ref.dtype)
    kn_ref[pl.ds(SWA_WINDOW, seq), :] = _halves_rms(
        k_ref[0].astype(jnp.float32), gk_ref[...]).astype(kn_ref.dtype)
    lane = lax.broadcasted_iota(jnp.int32, (1, LANES), 1)
    rel = (lax.broadcasted_iota(jnp.int32, (tq, tk), 0) + SWA_WINDOW
           - lax.broadcasted_iota(jnp.int32, (tq, tk), 1))
    in_window = (rel >= 0) & (rel < SWA_WINDOW)
    rel_f = rel.astype(jnp.float32)
    col_idx = lax.broadcasted_iota(jnp.int32, (tq, tk), 1)
    inv_sqrt_d = 1.0 / math.sqrt(SWA_HEAD_DIM)

    def q_tile(i, _):
        t0 = pl.multiple_of(i * tq, tq)
        kb = kn_ref[pl.ds(t0, tk), :]
        valid = in_window & (col_idx + t0 >= SWA_WINDOW)
        prev0 = pl.multiple_of(jnp.maximum(t0 - SWA_WINDOW, 0), SWA_WINDOW)
        v_prev = v_ref[0, pl.ds(prev0, SWA_WINDOW), :]
        vb = jnp.concatenate([v_prev, v_ref[0, pl.ds(t0, tq), :]], axis=0)
        for c in range(ncol):
            qn = _halves_rms(q_ref[0, pl.ds(t0, tq), c * LANES:(c + 1) * LANES].astype(jnp.float32),
                             gq_ref[...]).astype(jnp.bfloat16)
            o_col = None
            for half in range(2):
                in_half = (lane >= half * SWA_HEAD_DIM) & (lane < (half + 1) * SWA_HEAD_DIM)
                qm = jnp.where(in_half, qn, jnp.zeros_like(qn))
                s = lax.dot_general(qm, kb, (((1,), (1,)), ((), ())),
                                    preferred_element_type=jnp.float32) * inv_sqrt_d
                s = s - slope_ref[pair, half, c] * rel_f
                s = jnp.where(valid, s, -jnp.inf)
                sink = sink_ref[pair, half, c]
                m = jnp.maximum(jnp.max(s, axis=-1, keepdims=True), sink)
                e = jnp.exp(s - m)
                p = e / (jnp.sum(e, axis=-1, keepdims=True) + jnp.exp(sink - m))
                o_h = jnp.dot(p.astype(jnp.bfloat16), vb, preferred_element_type=jnp.float32)
                o_col = o_h if o_col is None else jnp.where(in_half, o_h, o_col)
            o_ref[0, pl.ds(t0, tq), c * LANES:(c + 1) * LANES] = o_col.astype(o_ref.dtype)
        return 0

    lax.fori_loop(0, seq // tq, q_tile, 0)


def swa_attention(proj3d, gq2, gk2, slopes, sinks, *, tq=128):
    b, s, _ = proj3d.shape
    npair = SWA_KV_HEADS // 2
    qw = SWA_WIDTH // npair
    kcol0 = 2 * SWA_WIDTH // LANES
    smem = pl.BlockSpec(memory_space=pltpu.SMEM)
    return pl.pallas_call(
        functools.partial(_swa_attn_kernel, tq=tq),
        grid=(b, npair),
        in_specs=[pl.BlockSpec((1, s, qw), lambda bi, p: (bi, 0, p)),
                  pl.BlockSpec((1, s, LANES), lambda bi, p: (bi, 0, kcol0 + p)),
                  pl.BlockSpec((1, s, LANES), lambda bi, p: (bi, 0, kcol0 + npair + p)),
                  pl.BlockSpec((1, LANES), lambda bi, p: (0, 0)),
                  pl.BlockSpec((1, LANES), lambda bi, p: (0, 0)),
                  smem, smem],
        out_specs=pl.BlockSpec((1, s, qw), lambda bi, p: (bi, 0, p)),
        out_shape=jax.ShapeDtypeStruct((b, s, SWA_WIDTH), jnp.bfloat16),
        scratch_shapes=[pltpu.VMEM((s + SWA_WINDOW, LANES), jnp.bfloat16)],
        compiler_params=_cparams("arbitrary", "arbitrary"),
        name="swa_attention",
    )(proj3d, proj3d, proj3d, gq2, gk2, slopes, sinks)


def _swa_head_order():
    order = []
    for p in range(SWA_KV_HEADS // 2):
        for c in range(SWA_GROUP):
            order += [2 * p * SWA_GROUP + c, (2 * p + 1) * SWA_GROUP + c]
    return order


def swa_layer(x, norm_g, w_in, q_head_norm, k_head_norm, sinks, w_out):
    b, s, d = x.shape
    x2d = x.reshape(b * s, d)
    kv_w = SWA_KV_HEADS * SWA_HEAD_DIM
    c1, c2, c3 = SWA_WIDTH, SWA_WIDTH + kv_w, SWA_WIDTH + 2 * kv_w
    order = jnp.asarray(_swa_head_order())
    perm = (order[:, None] * SWA_HEAD_DIM + jnp.arange(SWA_HEAD_DIM)[None, :]).reshape(-1)
    w_in_l = jnp.concatenate([w_in[:, :c1][:, perm], w_in[:, c3:][:, perm], w_in[:, c1:c3]], axis=1)
    w_out_l = w_out[perm, :]
    slopes = 2.0 ** (-8.0 * jnp.arange(1, SWA_HEADS + 1, dtype=jnp.float32) / SWA_HEADS)
    by_pos = lambda t: t[order].reshape(SWA_KV_HEADS // 2, SWA_GROUP, 2).transpose(0, 2, 1)
    gq2 = jnp.tile(q_head_norm, 2).reshape(1, LANES)
    gk2 = jnp.tile(k_head_norm, 2).reshape(1, LANES)
    proj = norm_matmul(x2d, norm_g, w_in_l.astype(jnp.bfloat16))
    o = swa_attention(proj.reshape(b, s, -1), gq2, gk2, by_pos(slopes),
                      by_pos(sinks.astype(jnp.float32)))
    y = gate_out_proj(o.reshape(b * s, SWA_WIDTH), proj, 1, x2d, w_out_l.astype(jnp.bfloat16))
    return y.reshape(b, s, d)


def kernel(x, l0_norm, l0_w_in, l0_w_out, l1_norm, l1_w_in, l1_q_a_norm, l1_w_uq, l1_kv_a_norm,
           l1_w_ukv, l1_q_head_norm, l1_k_head_norm, l1_w_out, l2_norm, l2_w_in, l2_q_head_norm,
           l2_k_head_norm, l2_sinks, l2_w_out, l3_norm, l3_w_in, l3_w_out):
    x = stick_breaking_layer(x, l0_norm, l0_w_in, l0_w_out)
    x = mla_layer(x, l1_norm, l1_w_in, l1_q_a_norm, l1_w_uq, l1_kv_a_norm, l1_w_ukv,
                  l1_q_head_norm, l1_k_head_norm, l1_w_out)
    x = swa_layer(x, l2_norm, l2_w_in, l2_q_head_norm, l2_k_head_norm, l2_sinks, l2_w_out)
    x = stick_breaking_layer(x, l3_norm, l3_w_in, l3_w_out)
    return x
```

```python
import functools
import math

import jax
import jax.numpy as jnp
from jax import lax
from jax.experimental import pallas as pl
from jax.experimental.pallas import tpu as pltpu

NORM_EPS = 1e-6
LOG2E = 1.4426950408889634

LANES = 128
VMEM_LIMIT_BYTES = 56 * 1024 * 1024

SB_HEADS = 16
SB_HEAD_DIM = 64
SB_WIDTH = SB_HEADS * SB_HEAD_DIM

MLA_HEADS = 8
MLA_NOPE_DIM = 128
MLA_ROPE_DIM = 64
MLA_QK_DIM = MLA_NOPE_DIM + MLA_ROPE_DIM
MLA_QK_PAD = 256
MLA_V_DIM = 128
MLA_Q_RANK = 256
MLA_KV_RANK = 128
MLA_WIDTH = MLA_HEADS * MLA_V_DIM
ROPE_THETA = 10000.0

SWA_HEADS = 16
SWA_KV_HEADS = 4
SWA_GROUP = SWA_HEADS // SWA_KV_HEADS
SWA_HEAD_DIM = 64
SWA_WINDOW = 128
SWA_WIDTH = SWA_HEADS * SWA_HEAD_DIM

F32_EXP2_ZERO_BELOW = -160.0
MASKED_LOGIT = 1e30


def _cparams(*sem):
    return pltpu.CompilerParams(dimension_semantics=sem, vmem_limit_bytes=VMEM_LIMIT_BYTES)


def _rms_scale(xf, n):
    return lax.rsqrt(jnp.sum(xf * xf, axis=-1, keepdims=True) * (1.0 / n) + NORM_EPS)


def _norm_matmul_kernel(x_ref, g_ref, w_ref, o_ref, *, n_chunk):
    xf = x_ref[...]
    xn = (xf * _rms_scale(xf, xf.shape[-1]) * g_ref[...]).astype(jnp.bfloat16)
    n_out = o_ref.shape[-1]
    for c in range(0, n_out, n_chunk):
        o_ref[:, c:c + n_chunk] = jnp.dot(
            xn, w_ref[:, c:c + n_chunk], preferred_element_type=jnp.float32).astype(o_ref.dtype)


def norm_matmul(x2d, g, w_bf16, *, tm=512, n_chunk=512):
    n, d = x2d.shape
    n_out = w_bf16.shape[1]
    return pl.pallas_call(
        functools.partial(_norm_matmul_kernel, n_chunk=n_chunk),
        grid=(n // tm,),
        in_specs=[pl.BlockSpec((tm, d), lambda i: (i, 0)),
                  pl.BlockSpec((1, d), lambda i: (0, 0)),
                  pl.BlockSpec((d, n_out), lambda i: (0, 0))],
        out_specs=pl.BlockSpec((tm, n_out), lambda i: (i, 0)),
        out_shape=jax.ShapeDtypeStruct((n, n_out), jnp.bfloat16),
        compiler_params=_cparams("arbitrary"),
        name="norm_matmul",
    )(x2d, g.reshape(1, d), w_bf16)


def _gate_out_proj_kernel(o_ref, gate_ref, x_ref, w_ref, y_ref):
    gate = gate_ref[...].astype(jnp.float32)
    og = (o_ref[...].astype(jnp.float32) * (gate * jax.nn.sigmoid(gate))).astype(jnp.bfloat16)
    y_ref[...] = x_ref[...] + jnp.dot(og, w_ref[...], preferred_element_type=jnp.float32)


def gate_out_proj(o2d, proj2d, gate_block, x2d, w_bf16, *, tm=512):
    n, width = o2d.shape
    d = x2d.shape[1]
    return pl.pallas_call(
        _gate_out_proj_kernel,
        grid=(n // tm,),
        in_specs=[pl.BlockSpec((tm, width), lambda i: (i, 0)),
                  pl.BlockSpec((tm, width), lambda i: (i, gate_block)),
                  pl.BlockSpec((tm, d), lambda i: (i, 0)),
                  pl.BlockSpec((width, d), lambda i: (0, 0))],
        out_specs=pl.BlockSpec((tm, d), lambda i: (i, 0)),
        out_shape=jax.ShapeDtypeStruct((n, d), jnp.float32),
        compiler_params=_cparams("arbitrary"),
        name="gate_out_proj",
    )(o2d, proj2d, x2d, w_bf16)


def _sb_scores(k_blk, qm):
    return [lax.dot_general(k_blk, q_h, (((1,), (1,)), ((), ())), preferred_element_type=jnp.float32)
            for q_h in qm]


def _sb_front(z, mask, carry, tri):
    z2 = z
    if mask is not None:
        z2 = jnp.where(mask, z2, -MASKED_LOGIT)
    soft = jnp.log2(1.0 + jnp.exp2(-jnp.abs(z2)))
    ls2 = jnp.minimum(z2, 0.0) - soft
    lf2 = ls2 - z2
    later = jnp.dot(tri, lf2.astype(jnp.bfloat16), preferred_element_type=jnp.float32)
    return (ls2 if carry is None else ls2 + carry), later, jnp.sum(lf2, axis=0, keepdims=True)


def _sb_back(arg, later, vt_blk):
    return jnp.dot(vt_blk, jnp.exp2(arg + later).astype(jnp.bfloat16),
                   preferred_element_type=jnp.float32)


def _transpose_rows_into(src_ref, dst_ref, chunk):
    def body(c, _):
        r0 = pl.multiple_of(c * chunk, chunk)
        dst_ref[:, pl.ds(r0, chunk)] = (
            src_ref[0, pl.ds(r0, chunk), :].astype(jnp.float32).T.astype(dst_ref.dtype))
        return 0
    lax.fori_loop(0, src_ref.shape[1] // chunk, body, 0)


def _sb_attn_kernel(q_ref, k_ref, v_ref, o_ref, vt_ref, arg_ref, later_ref, *, tile, head_dim):
    seq = q_ref.shape[1]
    n_tiles = seq // tile
    heads = range(LANES // head_dim)
    lane = lax.broadcasted_iota(jnp.int32, (1, LANES), 1)
    in_head = [(lane >= h * head_dim) & (lane < (h + 1) * head_dim) for h in heads]
    earlier = (lax.broadcasted_iota(jnp.int32, (tile, tile), 0)
               < lax.broadcasted_iota(jnp.int32, (tile, tile), 1))
    tri = earlier.astype(jnp.bfloat16)
    _transpose_rows_into(v_ref, vt_ref, tile)

    def rows(s0):
        return pl.ds(s0 if isinstance(s0, int) else pl.multiple_of(s0, tile), tile)

    def masked_q(t0):
        q = q_ref[0, rows(t0), :]
        return [jnp.where(m, q, jnp.zeros_like(q)) for m in in_head]

    def vt_tile(s0, h):
        return vt_ref[h * head_dim:(h + 1) * head_dim, rows(s0)]

    def blocks(z_kinds, masks, key_starts, acc):
        acc = None if acc is None else list(acc)
        carry = [None for _ in heads]
        for kind in range(2 if key_starts is not None else 0):
            for h in heads:
                b = 2 * kind + h
                acc[h] = acc[h] + _sb_back(arg_ref[b], later_ref[b], vt_tile(key_starts[kind], h))
        for kind in range(len(z_kinds) if z_kinds is not None else 0):
            for h in heads:
                b = 2 * kind + h
                arg, later, colsum = _sb_front(z_kinds[kind][h], masks[kind], carry[h], tri)
                arg_ref[b] = arg
                later_ref[b] = later
                carry[h] = colsum if carry[h] is None else carry[h] + colsum
        return acc, carry

    def write_out(t0, acc):
        o_ref[0, rows(t0), :] = jnp.concatenate(acc, axis=0).T.astype(o_ref.dtype)

    zero_acc = tuple(jnp.zeros((head_dim, tile), jnp.float32) for _ in heads)

    blocks([_sb_scores(k_ref[0, rows(0), :], masked_q(0))], [earlier], None, None)
    for h in heads:
        arg_ref[2 + h] = jnp.full((tile, tile), -MASKED_LOGIT, jnp.float32)
        later_ref[2 + h] = jnp.zeros((tile, tile), jnp.float32)

    def q_tile(i, acc_swept):
        t0 = i * tile
        qm = masked_q(t0)
        z_kinds = [_sb_scores(k_ref[0, rows(t0), :], qm), _sb_scores(k_ref[0, rows(t0 - tile), :], qm)]
        acc_prev, carry = blocks(z_kinds, [earlier, None],
                                 [t0 - tile, jnp.maximum(t0 - 2 * tile, 0)], acc_swept)
        write_out(t0 - tile, acc_prev)

        def alive(carry):
            return jnp.max(functools.reduce(jnp.maximum, carry)) >= F32_EXP2_ZERO_BELOW

        def cond(state):
            jj, live, _, _ = state
            return (jj <= i) & live

        def sweep(state):
            jj, _, acc, carry = state
            s0 = t0 - jj * tile
            z = _sb_scores(k_ref[0, rows(s0), :], qm)
            acc, carry = list(acc), list(carry)
            for h in heads:
                arg, later, colsum = _sb_front(z[h], None, carry[h], tri)
                acc[h] = acc[h] + _sb_back(arg, later, vt_tile(s0, h))
                carry[h] = carry[h] + colsum
            return jj + 1, alive(carry), tuple(acc), tuple(carry)

        _, _, acc, _ = lax.while_loop(cond, sweep, (jnp.int32(2), alive(carry), zero_acc,
                                                    tuple(carry)))
        return acc

    acc_swept = lax.fori_loop(1, n_tiles, q_tile, zero_acc)
    last = (n_tiles - 1) * tile
    write_out(last, blocks(None, None, [last, max(last - tile, 0)], acc_swept)[0])


def sb_attention(proj3d, *, tile=256):
    b, s, _ = proj3d.shape
    cols = SB_WIDTH // LANES
    blk = lambda off: pl.BlockSpec((1, s, LANES), lambda bi, p: (bi, 0, off + p))
    return pl.pallas_call(
        functools.partial(_sb_attn_kernel, tile=tile, head_dim=SB_HEAD_DIM),
        grid=(b, cols),
        in_specs=[blk(0), blk(cols), blk(2 * cols)],
        out_specs=pl.BlockSpec((1, s, LANES), lambda bi, p: (bi, 0, p)),
        out_shape=jax.ShapeDtypeStruct((b, s, SB_WIDTH), jnp.bfloat16),
        scratch_shapes=[pltpu.VMEM((LANES, s), jnp.bfloat16),
                        pltpu.VMEM((2 * LANES // SB_HEAD_DIM, tile, tile), jnp.float32),
                        pltpu.VMEM((2 * LANES // SB_HEAD_DIM, tile, tile), jnp.float32)],
        compiler_params=_cparams("arbitrary", "arbitrary"),
        name="sb_attention",
    )(proj3d, proj3d, proj3d)


def stick_breaking_layer(x, norm_g, w_in, w_out):
    b, s, d = x.shape
    x2d = x.reshape(b * s, d)
    w_in_l = w_in.at[:, :SB_WIDTH].multiply(LOG2E / math.sqrt(SB_HEAD_DIM))
    proj = norm_matmul(x2d, norm_g, w_in_l.astype(jnp.bfloat16))
    o = sb_attention(proj.reshape(b, s, -1))
    y = gate_out_proj(o.reshape(b * s, SB_WIDTH), proj, 3, x2d, w_out.astype(jnp.bfloat16))
    return y.reshape(b, s, d)


def _rope_tables(seq):
    half = MLA_ROPE_DIM // 2
    inv_freq = ROPE_THETA ** (-jnp.arange(half, dtype=jnp.float32) / half)
    ang = jnp.arange(seq, dtype=jnp.float32)[:, None] * inv_freq[None, :]
    cos, sin = jnp.cos(ang), jnp.sin(ang)
    zero = jnp.zeros((seq, LANES - MLA_ROPE_DIM), jnp.float32)
    return (jnp.concatenate([cos, cos, zero], axis=1),
            jnp.concatenate([-sin, sin, zero], axis=1))


def _rotate_half(pe):
    lane = lax.broadcasted_iota(jnp.int32, pe.shape, 1)
    half = MLA_ROPE_DIM // 2
    return jnp.where(lane < half, pltpu.roll(pe, LANES - half, axis=1), pltpu.roll(pe, half, axis=1))


def _mla_prep_kernel(qlat_ref, kvl_ref, cos_ref, sin_ref, qa_ref, kva_ref, gq_ref, gk_ref,
                     wuq_ref, wukv_ref, q_ref, k_ref, v_ref):
    cos, sin = cos_ref[...], sin_ref[...]
    ql = qlat_ref[...].astype(jnp.float32)
    qn = (ql * _rms_scale(ql, MLA_Q_RANK) * qa_ref[...]).astype(jnp.bfloat16)
    kvl = kvl_ref[...].astype(jnp.float32)
    kv_lat = kvl[:, :MLA_KV_RANK]
    kvn = (kv_lat * _rms_scale(kv_lat, MLA_KV_RANK) * kva_ref[...]).astype(jnp.bfloat16)
    gq, gk = gq_ref[...], gk_ref[...]
    k_pe = kvl[:, MLA_KV_RANK:]
    k_pe_sq = jnp.sum(k_pe * k_pe, axis=-1, keepdims=True)
    k_pe_g = k_pe * gk[:, MLA_NOPE_DIM:]
    k_pe_rot = k_pe_g * cos + _rotate_half(k_pe_g) * sin
    for h in range(MLA_HEADS):
        c0 = h * MLA_QK_PAD
        qh = jnp.dot(qn, wuq_ref[:, c0:c0 + MLA_QK_PAD], preferred_element_type=jnp.float32)
        qh = qh * _rms_scale(qh, MLA_QK_DIM) * gq
        q_pe = qh[:, MLA_NOPE_DIM:]
        q_ref[:, c0:c0 + MLA_NOPE_DIM] = qh[:, :MLA_NOPE_DIM].astype(q_ref.dtype)
        q_ref[:, c0 + MLA_NOPE_DIM:c0 + MLA_QK_PAD] = (
            q_pe * cos + _rotate_half(q_pe) * sin).astype(q_ref.dtype)
        kvh = jnp.dot(kvn, wukv_ref[:, c0:c0 + MLA_QK_PAD], preferred_element_type=jnp.float32)
        k_nope = kvh[:, :MLA_NOPE_DIM]
        r = lax.rsqrt((jnp.sum(k_nope * k_nope, axis=-1, keepdims=True) + k_pe_sq)
                      * (1.0 / MLA_QK_DIM) + NORM_EPS)
        k_ref[:, c0:c0 + MLA_NOPE_DIM] = (k_nope * r * gk[:, :MLA_NOPE_DIM]).astype(k_ref.dtype)
        k_ref[:, c0 + MLA_NOPE_DIM:c0 + MLA_QK_PAD] = (k_pe_rot * r).astype(k_ref.dtype)
        v_ref[:, h * MLA_V_DIM:(h + 1) * MLA_V_DIM] = kvh[:, MLA_NOPE_DIM:].astype(v_ref.dtype)


def mla_prep(proj2d, seq, q_a_norm, kv_a_norm, q_head_norm, k_head_norm, wuq_pad, wukv, *, tm=512):
    n = proj2d.shape[0]
    cos, sin = _rope_tables(seq)
    pad = jnp.zeros((MLA_QK_PAD - MLA_QK_DIM,), jnp.float32)
    gq = jnp.concatenate([q_head_norm * (LOG2E / math.sqrt(MLA_QK_DIM)), pad]).reshape(1, MLA_QK_PAD)
    gk = jnp.concatenate([k_head_norm, pad]).reshape(1, MLA_QK_PAD)
    qlat_blk = MLA_WIDTH // MLA_Q_RANK
    tiles_per_seq = seq // tm
    row = lambda i: (i, 0)
    const = lambda i: (0, 0)
    pos = lambda i: (i % tiles_per_seq, 0)
    return pl.pallas_call(
        _mla_prep_kernel,
        grid=(n // tm,),
        in_specs=[pl.BlockSpec((tm, MLA_Q_RANK), lambda i: (i, qlat_blk)),
                  pl.BlockSpec((tm, MLA_Q_RANK), lambda i: (i, qlat_blk + 1)),
                  pl.BlockSpec((tm, LANES), pos),
                  pl.BlockSpec((tm, LANES), pos),
                  pl.BlockSpec((1, MLA_Q_RANK), const),
                  pl.BlockSpec((1, MLA_KV_RANK), const),
                  pl.BlockSpec((1, MLA_QK_PAD), const),
                  pl.BlockSpec((1, MLA_QK_PAD), const),
                  pl.BlockSpec((MLA_Q_RANK, MLA_HEADS * MLA_QK_PAD), const),
                  pl.BlockSpec((MLA_KV_RANK, MLA_HEADS * MLA_QK_PAD), const)],
        out_specs=[pl.BlockSpec((tm, MLA_HEADS * MLA_QK_PAD), row),
                   pl.BlockSpec((tm, MLA_HEADS * MLA_QK_PAD), row),
                   pl.BlockSpec((tm, MLA_WIDTH), row)],
        out_shape=[jax.ShapeDtypeStruct((n, MLA_HEADS * MLA_QK_PAD), jnp.bfloat16),
                   jax.ShapeDtypeStruct((n, MLA_HEADS * MLA_QK_PAD), jnp.bfloat16),
                   jax.ShapeDtypeStruct((n, MLA_WIDTH), jnp.bfloat16)],
        compiler_params=_cparams("arbitrary"),
        name="mla_prep",
    )(proj2d, proj2d, cos, sin, q_a_norm.reshape(1, -1), kv_a_norm.reshape(1, -1), gq, gk,
      wuq_pad, wukv)


def _causal_attn_kernel(q_ref, k_ref, v_ref, o_ref, vt_ref, s_ref, p_ref, *, tile):
    seq = q_ref.shape[1]
    chains = range(2)
    group = 2 * tile
    causal = (lax.broadcasted_iota(jnp.int32, (tile, tile), 0)
              <= lax.broadcasted_iota(jnp.int32, (tile, tile), 1))
    _transpose_rows_into(v_ref, vt_ref, tile)

    def qk(q, s0):
        return lax.dot_general(k_ref[0, pl.ds(pl.multiple_of(s0, tile), tile), :], q,
                               (((1,), (1,)), ((), ())), preferred_element_type=jnp.float32)

    def pv(p, s0):
        return jnp.dot(vt_ref[:, pl.ds(pl.multiple_of(s0, tile), tile)], p,
                       preferred_element_type=jnp.float32)

    def softmax_step(s2, m, l, mask):
        if mask is not None:
            s2 = jnp.where(mask, s2, -jnp.inf)
        m_new = jnp.maximum(m, jnp.max(s2, axis=0, keepdims=True))
        p = jnp.exp2(s2 - m_new)
        alpha = jnp.exp2(m - m_new)
        return m_new, alpha * l + jnp.sum(p, axis=0, keepdims=True), alpha, p.astype(jnp.bfloat16)

    def q_group(i, _):
        t0 = pl.multiple_of(i * group, group)
        q = [q_ref[0, pl.ds(t0 + c * tile, tile), :] for c in chains]
        for c in chains:
            s_ref[c] = qk(q[c], 0)
            p_ref[c] = jnp.zeros((tile, tile), p_ref.dtype)
        init = tuple((jnp.full((1, tile), -jnp.inf, jnp.float32), jnp.zeros((1, tile), jnp.float32),
                      jnp.zeros((v_ref.shape[-1], tile), jnp.float32),
                      jnp.ones((1, tile), jnp.float32)) for _ in chains)

        def first_half(j, state, s_b_chains, masks):
            b_prev = jnp.maximum(j - 1, 0) * group + tile
            pv_b = [pv(p_ref[c], b_prev) for c in chains]
            s_b = {c: qk(q[c], j * group + tile) for c in s_b_chains}
            out = []
            for c in chains:
                m, l, acc, alpha_b = state[c]
                acc = alpha_b * acc + pv_b[c]
                m, l, alpha_a, p_a = softmax_step(s_ref[c], m, l, masks[c])
                out.append((m, l, acc, alpha_a, p_a))
            return out, s_b

        def full_tiles(j, state):
            mid, s_b = first_half(j, state, chains, (None, None))
            pv_a = [pv(mid[c][4], j * group) for c in chains]
            for c in chains:
                s_ref[c] = qk(q[c], (j + 1) * group)
            out = []
            for c in chains:
                m, l, acc, alpha_a, _ = mid[c]
                acc = alpha_a * acc + pv_a[c]
                m, l, alpha_b, p_b = softmax_step(s_b[c], m, l, None)
                p_ref[c] = p_b
                out.append((m, l, acc, alpha_b))
            return tuple(out)

        state = lax.fori_loop(0, i, full_tiles, init)
        mid, s_b = first_half(i, state, (1,), (causal, None))
        pv_a = [pv(mid[c][4], t0) for c in chains]
        m, l, acc, alpha_a, _ = mid[0]
        o_ref[0, pl.ds(t0, tile), :] = ((alpha_a * acc + pv_a[0]) / l).T.astype(o_ref.dtype)
        m, l, acc, alpha_a, _ = mid[1]
        acc = alpha_a * acc + pv_a[1]
        m, l, alpha_b, p_b = softmax_step(s_b[1], m, l, causal)
        acc = alpha_b * acc + pv(p_b, t0 + tile)
        o_ref[0, pl.ds(t0 + tile, tile), :] = (acc / l).T.astype(o_ref.dtype)
        return 0

    lax.fori_loop(0, seq // group, q_group, 0)


def mla_attention(q3d, k3d, v3d, *, tile=256):
    b, s, _ = q3d.shape
    qk = lambda bi, h: (bi, 0, h)
    return pl.pallas_call(
        functools.partial(_causal_attn_kernel, tile=tile),
        grid=(b, MLA_HEADS),
        in_specs=[pl.BlockSpec((1, s, MLA_QK_PAD), qk),
                  pl.BlockSpec((1, s, MLA_QK_PAD), qk),
                  pl.BlockSpec((1, s, MLA_V_DIM), qk)],
        out_specs=pl.BlockSpec((1, s, MLA_V_DIM), qk),
        out_shape=jax.ShapeDtypeStruct((b, s, MLA_WIDTH), jnp.bfloat16),
        scratch_shapes=[pltpu.VMEM((MLA_V_DIM, s), jnp.bfloat16),
                        pltpu.VMEM((2, tile, tile), jnp.float32),
                        pltpu.VMEM((2, tile, tile), jnp.bfloat16)],
        compiler_params=_cparams("arbitrary", "arbitrary"),
        name="mla_attention",
    )(q3d, k3d, v3d)


def mla_layer(x, norm_g, w_in, q_a_norm, w_uq, kv_a_norm, w_ukv, q_head_norm, k_head_norm, w_out):
    b, s, d = x.shape
    x2d = x.reshape(b * s, d)
    c1 = MLA_Q_RANK
    c3 = c1 + MLA_KV_RANK + MLA_ROPE_DIM
    w_in_l = jnp.concatenate(
        [w_in[:, c3:], w_in[:, :c3], jnp.zeros((d, LANES - MLA_ROPE_DIM), w_in.dtype)], axis=1)
    wuq_pad = jnp.pad(w_uq.reshape(MLA_Q_RANK, MLA_HEADS, MLA_QK_DIM),
                      ((0, 0), (0, 0), (0, MLA_QK_PAD - MLA_QK_DIM))).reshape(MLA_Q_RANK, -1)
    proj = norm_matmul(x2d, norm_g, w_in_l.astype(jnp.bfloat16))
    q, k, v = mla_prep(proj, s, q_a_norm, kv_a_norm, q_head_norm, k_head_norm,
                       wuq_pad.astype(jnp.bfloat16), w_ukv.astype(jnp.bfloat16))
    o = mla_attention(q.reshape(b, s, -1), k.reshape(b, s, -1), v.reshape(b, s, -1))
    y = gate_out_proj(o.reshape(b * s, MLA_WIDTH), proj, 0, x2d, w_out.astype(jnp.bfloat16))
    return y.reshape(b, s, d)


def _halves_rms(xf, gain):
    lane = lax.broadcasted_iota(jnp.int32, xf.shape, 1)
    lo = lane < SWA_HEAD_DIM
    sq = xf * xf
    s_lo = jnp.sum(jnp.where(lo, sq, 0.0), axis=-1, keepdims=True)
    s_hi = jnp.sum(jnp.where(lo, 0.0, sq), axis=-1, keepdims=True)
    r = lax.rsqrt(jnp.where(lo, s_lo, s_hi) * (1.0 / SWA_HEAD_DIM) + NORM_EPS)
    return xf * r * gain


def _swa_attn_kernel(q_ref, k_ref, v_ref, gq_ref, gk_ref, slope_ref, sink_ref, o_ref,
                     kn_ref, vt_ref, bias_ref, *, tq):
    seq = q_ref.shape[1]
    ncol = q_ref.shape[2] // LANES
    halves = range(LANES // SWA_HEAD_DIM)
    tk = tq + SWA_WINDOW
    pair = pl.program_id(1)
    kn_ref[pl.ds(0, SWA_WINDOW), :] = jnp.zeros((SWA_WINDOW, LANES), kn_ref.dtype)
    kn_ref[pl.ds(SWA_WINDOW, seq), :] = _halves_rms(
        k_ref[0].astype(jnp.float32), gk_ref[...]).astype(kn_ref.dtype)
    vt_ref[:, pl.ds(0, SWA_WINDOW)] = jnp.zeros((LANES, SWA_WINDOW), vt_ref.dtype)
    _transpose_rows_into(v_ref, vt_ref.at[:, pl.ds(SWA_WINDOW, seq)], SWA_WINDOW)
    lane = lax.broadcasted_iota(jnp.int32, (1, LANES), 1)
    in_half = [(lane >= h * SWA_HEAD_DIM) & (lane < (h + 1) * SWA_HEAD_DIM) for h in halves]
    key_row = lax.broadcasted_iota(jnp.int32, (tk, tq), 0)
    rel = lax.broadcasted_iota(jnp.int32, (tk, tq), 1) + SWA_WINDOW - key_row
    in_window = (rel >= 0) & (rel < SWA_WINDOW)
    rel_f = rel.astype(jnp.float32)
    for h in halves:
        for c in range(ncol):
            bias_ref[h, :, c * tq:(c + 1) * tq] = jnp.where(
                in_window, -slope_ref[pair, h, c] * rel_f, -jnp.inf)
    sink_row = [jnp.concatenate([jnp.full((1, tq), sink_ref[pair, h, c], jnp.float32)
                                 for c in range(ncol)], axis=1) for h in halves]
    before_start = jnp.concatenate([key_row < SWA_WINDOW] * ncol, axis=1)
    q_scale = 1.0 / math.sqrt(SWA_HEAD_DIM)

    def q_tile(i, first):
        t0 = pl.multiple_of(i * tq, tq)
        kb = kn_ref[pl.ds(t0, tk), :]
        qn = [(_halves_rms(q_ref[0, pl.ds(t0, tq), c * LANES:(c + 1) * LANES].astype(jnp.float32),
                           gq_ref[...]) * q_scale).astype(jnp.bfloat16) for c in range(ncol)]
        s_t = [lax.dot_general(
            kb, jnp.concatenate([jnp.where(in_half[h], qn[c], jnp.zeros_like(qn[c]))
                                 for c in range(ncol)], axis=0),
            (((1,), (1,)), ((), ())), preferred_element_type=jnp.float32) for h in halves]
        o_t = []
        for h in halves:
            s = s_t[h] + bias_ref[h]
            if first:
                s = jnp.where(before_start, -jnp.inf, s)
            m = jnp.maximum(jnp.max(s, axis=0, keepdims=True), sink_row[h])
            e = jnp.exp(s - m)
            denom = jnp.sum(e, axis=0, keepdims=True) + jnp.exp(sink_row[h] - m)
            pv = jnp.dot(vt_ref[h * SWA_HEAD_DIM:(h + 1) * SWA_HEAD_DIM, pl.ds(t0, tk)],
                         e.astype(jnp.bfloat16), preferred_element_type=jnp.float32)
            o_t.append(pv / denom)
        for c in range(ncol):
            o_col_t = jnp.concatenate([o_t[h][:, c * tq:(c + 1) * tq] for h in halves], axis=0)
            o_ref[0, pl.ds(t0, tq), c * LANES:(c + 1) * LANES] = o_col_t.T.astype(o_ref.dtype)

    q_tile(0, True)

    def q_tile_loop(i, _):
        q_tile(i, False)
        return 0

    lax.fori_loop(1, seq // tq, q_tile_loop, 0)


def swa_attention(proj3d, gq2, gk2, slopes, sinks, *, tq=128):
    b, s, _ = proj3d.shape
    npair = SWA_KV_HEADS // 2
    qw = SWA_WIDTH // npair
    kcol0 = 2 * SWA_WIDTH // LANES
    smem = pl.BlockSpec(memory_space=pltpu.SMEM)
    return pl.pallas_call(
        functools.partial(_swa_attn_kernel, tq=tq),
        grid=(b, npair),
        in_specs=[pl.BlockSpec((1, s, qw), lambda bi, p: (bi, 0, p)),
                  pl.BlockSpec((1, s, LANES), lambda bi, p: (bi, 0, kcol0 + p)),
                  pl.BlockSpec((1, s, LANES), lambda bi, p: (bi, 0, kcol0 + npair + p)),
                  pl.BlockSpec((1, LANES), lambda bi, p: (0, 0)),
                  pl.BlockSpec((1, LANES), lambda bi, p: (0, 0)),
                  smem, smem],
        out_specs=pl.BlockSpec((1, s, qw), lambda bi, p: (bi, 0, p)),
        out_shape=jax.ShapeDtypeStruct((b, s, SWA_WIDTH), jnp.bfloat16),
        scratch_shapes=[pltpu.VMEM((s + SWA_WINDOW, LANES), jnp.bfloat16),
                        pltpu.VMEM((LANES, s + SWA_WINDOW), jnp.bfloat16),
                        pltpu.VMEM((LANES // SWA_HEAD_DIM, tq + SWA_WINDOW, qw // LANES * tq),
                                   jnp.float32)],
        compiler_params=_cparams("arbitrary", "arbitrary"),
        name="swa_attention",
    )(proj3d, proj3d, proj3d, gq2, gk2, slopes, sinks)


def _swa_head_order():
    order = []
    for p in range(SWA_KV_HEADS // 2):
        for c in range(SWA_GROUP):
            order += [2 * p * SWA_GROUP + c, (2 * p + 1) * SWA_GROUP + c]
    return order


def swa_layer(x, norm_g, w_in, q_head_norm, k_head_norm, sinks, w_out):
    b, s, d = x.shape
    x2d = x.reshape(b * s, d)
    kv_w = SWA_KV_HEADS * SWA_HEAD_DIM
    c1, c2, c3 = SWA_WIDTH, SWA_WIDTH + kv_w, SWA_WIDTH + 2 * kv_w
    order = jnp.asarray(_swa_head_order())
    perm = (order[:, None] * SWA_HEAD_DIM + jnp.arange(SWA_HEAD_DIM)[None, :]).reshape(-1)
    w_in_l = jnp.concatenate([w_in[:, :c1][:, perm], w_in[:, c3:][:, perm], w_in[:, c1:c3]], axis=1)
    w_out_l = w_out[perm, :]
    slopes = 2.0 ** (-8.0 * jnp.arange(1, SWA_HEADS + 1, dtype=jnp.float32) / SWA_HEADS)
    by_pos = lambda t: t[order].reshape(SWA_KV_HEADS // 2, SWA_GROUP, 2).transpose(0, 2, 1)
    gq2 = jnp.tile(q_head_norm, 2).reshape(1, LANES)
    gk2 = jnp.tile(k_head_norm, 2).reshape(1, LANES)
    proj = norm_matmul(x2d, norm_g, w_in_l.astype(jnp.bfloat16))
    o = swa_attention(proj.reshape(b, s, -1), gq2, gk2, by_pos(slopes),
                      by_pos(sinks.astype(jnp.float32)))
    y = gate_out_proj(o.reshape(b * s, SWA_WIDTH), proj, 1, x2d, w_out_l.astype(jnp.bfloat16))
    return y.reshape(b, s, d)


def kernel(x, l0_norm, l0_w_in, l0_w_out, l1_norm, l1_w_in, l1_q_a_norm, l1_w_uq, l1_kv_a_norm,
           l1_w_ukv, l1_q_head_norm, l1_k_head_norm, l1_w_out, l2_norm, l2_w_in, l2_q_head_norm,
           l2_k_head_norm, l2_sinks, l2_w_out, l3_norm, l3_w_in, l3_w_out):
    x = stick_breaking_layer(x, l0_norm, l0_w_in, l0_w_out)
    x = mla_layer(x, l1_norm, l1_w_in, l1_q_a_norm, l1_w_uq, l1_kv_a_norm, l1_w_ukv,
                  l1_q_head_norm, l1_k_head_norm, l1_w_out)
    x = swa_layer(x, l2_norm, l2_w_in, l2_q_head_norm, l2_k_head_norm, l2_sinks, l2_w_out)
    x = stick_breaking_layer(x, l3_norm, l3_w_in, l3_w_out)
    return x
```

```python
import functools
import math

import jax
import jax.numpy as jnp
from jax import lax
from jax.experimental import pallas as pl
from jax.experimental.pallas import tpu as pltpu

NORM_EPS = 1e-6
LOG2E = 1.4426950408889634

LANES = 128
VMEM_LIMIT_BYTES = 56 * 1024 * 1024

SB_HEADS = 16
SB_HEAD_DIM = 64
SB_WIDTH = SB_HEADS * SB_HEAD_DIM

MLA_HEADS = 8
MLA_NOPE_DIM = 128
MLA_ROPE_DIM = 64
MLA_QK_DIM = MLA_NOPE_DIM + MLA_ROPE_DIM
MLA_QK_PAD = 256
MLA_V_DIM = 128
MLA_Q_RANK = 256
MLA_KV_RANK = 128
MLA_WIDTH = MLA_HEADS * MLA_V_DIM
ROPE_THETA = 10000.0

SWA_HEADS = 16
SWA_KV_HEADS = 4
SWA_GROUP = SWA_HEADS // SWA_KV_HEADS
SWA_HEAD_DIM = 64
SWA_WINDOW = 128
SWA_WIDTH = SWA_HEADS * SWA_HEAD_DIM

F32_EXP2_ZERO_BELOW = -160.0
MASKED_LOGIT = 1e30


def _cparams(*sem):
    return pltpu.CompilerParams(dimension_semantics=sem, vmem_limit_bytes=VMEM_LIMIT_BYTES)


def _rms_scale(xf, n):
    return lax.rsqrt(jnp.sum(xf * xf, axis=-1, keepdims=True) * (1.0 / n) + NORM_EPS)


def _norm_matmul_kernel(x_ref, g_ref, w_ref, o_ref, *vt_ref, n_chunk, vt_cols):
    xf = x_ref[...]
    xn = (xf * _rms_scale(xf, xf.shape[-1]) * g_ref[...]).astype(jnp.bfloat16)
    n_out = o_ref.shape[-1]
    for c in range(0, n_out, n_chunk):
        acc = jnp.dot(xn, w_ref[:, c:c + n_chunk], preferred_element_type=jnp.float32)
        o_ref[:, c:c + n_chunk] = acc.astype(o_ref.dtype)
        if vt_cols is not None:
            lo, hi = max(c, vt_cols[0]), min(c + n_chunk, vt_cols[0] + vt_cols[1])
            if lo < hi:
                vt_ref[0][0, lo - vt_cols[0]:hi - vt_cols[0], :] = (
                    acc[:, lo - c:hi - c].T.astype(vt_ref[0].dtype))


def norm_matmul(x2d, g, w_bf16, *, seq=None, vt_cols=None, tm=512, n_chunk=512):
    n, d = x2d.shape
    n_out = w_bf16.shape[1]
    out_specs = [pl.BlockSpec((tm, n_out), lambda i: (i, 0))]
    out_shape = [jax.ShapeDtypeStruct((n, n_out), jnp.bfloat16)]
    if vt_cols is not None:
        tiles_per_seq = seq // tm
        out_specs.append(pl.BlockSpec((1, vt_cols[1], tm),
                                      lambda i: (i // tiles_per_seq, 0, i % tiles_per_seq)))
        out_shape.append(jax.ShapeDtypeStruct((n // seq, vt_cols[1], seq), jnp.bfloat16))
    return pl.pallas_call(
        functools.partial(_norm_matmul_kernel, n_chunk=n_chunk, vt_cols=vt_cols),
        grid=(n // tm,),
        in_specs=[pl.BlockSpec((tm, d), lambda i: (i, 0)),
                  pl.BlockSpec((1, d), lambda i: (0, 0)),
                  pl.BlockSpec((d, n_out), lambda i: (0, 0))],
        out_specs=out_specs,
        out_shape=out_shape,
        compiler_params=_cparams("arbitrary"),
        name="norm_matmul",
    )(x2d, g.reshape(1, d), w_bf16)


def _gate_out_proj_kernel(o_ref, gate_ref, x_ref, w_ref, y_ref):
    gate = gate_ref[...].astype(jnp.float32)
    og = (o_ref[...].astype(jnp.float32) * (gate * jax.nn.sigmoid(gate))).astype(jnp.bfloat16)
    y_ref[...] = x_ref[...] + jnp.dot(og, w_ref[...], preferred_element_type=jnp.float32)


def gate_out_proj(o2d, proj2d, gate_block, x2d, w_bf16, *, tm=512):
    n, width = o2d.shape
    d = x2d.shape[1]
    return pl.pallas_call(
        _gate_out_proj_kernel,
        grid=(n // tm,),
        in_specs=[pl.BlockSpec((tm, width), lambda i: (i, 0)),
                  pl.BlockSpec((tm, width), lambda i: (i, gate_block)),
                  pl.BlockSpec((tm, d), lambda i: (i, 0)),
                  pl.BlockSpec((width, d), lambda i: (0, 0))],
        out_specs=pl.BlockSpec((tm, d), lambda i: (i, 0)),
        out_shape=jax.ShapeDtypeStruct((n, d), jnp.float32),
        compiler_params=_cparams("arbitrary"),
        name="gate_out_proj",
    )(o2d, proj2d, x2d, w_bf16)


def _sb_scores(k_blk, qm):
    return [lax.dot_general(k_blk, q_h, (((1,), (1,)), ((), ())), preferred_element_type=jnp.float32)
            for q_h in qm]


def _sb_front(z, mask, carry, tri):
    z2 = z
    if mask is not None:
        z2 = jnp.where(mask, z2, -MASKED_LOGIT)
    soft = jnp.log2(1.0 + jnp.exp2(-jnp.abs(z2)))
    ls2 = jnp.minimum(z2, 0.0) - soft
    lf2 = ls2 - z2
    later = jnp.dot(tri, lf2.astype(jnp.bfloat16), preferred_element_type=jnp.float32)
    return (ls2 if carry is None else ls2 + carry), later, jnp.sum(lf2, axis=0, keepdims=True)


def _sb_back(arg, later, vt_blk):
    return jnp.dot(vt_blk, jnp.exp2(arg + later).astype(jnp.bfloat16),
                   preferred_element_type=jnp.float32)


def _sb_attn_kernel(q_ref, k_ref, vt_ref, o_ref, arg_ref, later_ref, *, tile, head_dim):
    seq = q_ref.shape[1]
    n_tiles = seq // tile
    heads = range(LANES // head_dim)
    lane = lax.broadcasted_iota(jnp.int32, (1, LANES), 1)
    in_head = [(lane >= h * head_dim) & (lane < (h + 1) * head_dim) for h in heads]
    earlier = (lax.broadcasted_iota(jnp.int32, (tile, tile), 0)
               < lax.broadcasted_iota(jnp.int32, (tile, tile), 1))
    tri = earlier.astype(jnp.bfloat16)

    def rows(s0):
        return pl.ds(s0 if isinstance(s0, int) else pl.multiple_of(s0, tile), tile)

    def masked_q(t0):
        q = q_ref[0, rows(t0), :]
        return [jnp.where(m, q, jnp.zeros_like(q)) for m in in_head]

    def vt_tile(s0, h):
        return vt_ref[0, h * head_dim:(h + 1) * head_dim, rows(s0)]

    def blocks(z_kinds, masks, key_starts, acc):
        acc = None if acc is None else list(acc)
        carry = [None for _ in heads]
        for kind in range(2 if key_starts is not None else 0):
            for h in heads:
                b = 2 * kind + h
                acc[h] = acc[h] + _sb_back(arg_ref[b], later_ref[b], vt_tile(key_starts[kind], h))
        for kind in range(len(z_kinds) if z_kinds is not None else 0):
            for h in heads:
                b = 2 * kind + h
                arg, later, colsum = _sb_front(z_kinds[kind][h], masks[kind], carry[h], tri)
                arg_ref[b] = arg
                later_ref[b] = later
                carry[h] = colsum if carry[h] is None else carry[h] + colsum
        return acc, carry

    def write_out(t0, acc):
        o_ref[0, rows(t0), :] = jnp.concatenate(acc, axis=0).T.astype(o_ref.dtype)

    zero_acc = tuple(jnp.zeros((head_dim, tile), jnp.float32) for _ in heads)

    blocks([_sb_scores(k_ref[0, rows(0), :], masked_q(0))], [earlier], None, None)
    for h in heads:
        arg_ref[2 + h] = jnp.full((tile, tile), -MASKED_LOGIT, jnp.float32)
        later_ref[2 + h] = jnp.zeros((tile, tile), jnp.float32)

    def q_tile(i, acc_swept):
        t0 = i * tile
        qm = masked_q(t0)
        z_kinds = [_sb_scores(k_ref[0, rows(t0), :], qm), _sb_scores(k_ref[0, rows(t0 - tile), :], qm)]
        acc_prev, carry = blocks(z_kinds, [earlier, None],
                                 [t0 - tile, jnp.maximum(t0 - 2 * tile, 0)], acc_swept)
        write_out(t0 - tile, acc_prev)

        def alive(carry):
            return jnp.max(functools.reduce(jnp.maximum, carry)) >= F32_EXP2_ZERO_BELOW

        def cond(state):
            jj, live, _, _ = state
            return (jj <= i) & live

        def sweep(state):
            jj, _, acc, carry = state
            s0 = t0 - jj * tile
            z = _sb_scores(k_ref[0, rows(s0), :], qm)
            acc, carry = list(acc), list(carry)
            for h in heads:
                arg, later, colsum = _sb_front(z[h], None, carry[h], tri)
                acc[h] = acc[h] + _sb_back(arg, later, vt_tile(s0, h))
                carry[h] = carry[h] + colsum
            return jj + 1, alive(carry), tuple(acc), tuple(carry)

        _, _, acc, _ = lax.while_loop(cond, sweep, (jnp.int32(2), alive(carry), zero_acc,
                                                    tuple(carry)))
        return acc

    acc_swept = lax.fori_loop(1, n_tiles, q_tile, zero_acc)
    last = (n_tiles - 1) * tile
    write_out(last, blocks(None, None, [last, max(last - tile, 0)], acc_swept)[0])


def sb_attention(proj3d, vt3d, *, tile=256):
    b, s, _ = proj3d.shape
    cols = SB_WIDTH // LANES
    blk = lambda off: pl.BlockSpec((1, s, LANES), lambda bi, p: (bi, 0, off + p))
    return pl.pallas_call(
        functools.partial(_sb_attn_kernel, tile=tile, head_dim=SB_HEAD_DIM),
        grid=(b, cols),
        in_specs=[blk(0), blk(cols), pl.BlockSpec((1, LANES, s), lambda bi, p: (bi, p, 0))],
        out_specs=pl.BlockSpec((1, s, LANES), lambda bi, p: (bi, 0, p)),
        out_shape=jax.ShapeDtypeStruct((b, s, SB_WIDTH), jnp.bfloat16),
        scratch_shapes=[pltpu.VMEM((2 * LANES // SB_HEAD_DIM, tile, tile), jnp.float32),
                        pltpu.VMEM((2 * LANES // SB_HEAD_DIM, tile, tile), jnp.float32)],
        compiler_params=_cparams("arbitrary", "arbitrary"),
        name="sb_attention",
    )(proj3d, proj3d, vt3d)


def stick_breaking_layer(x, norm_g, w_in, w_out):
    b, s, d = x.shape
    x2d = x.reshape(b * s, d)
    w_in_l = w_in.at[:, :SB_WIDTH].multiply(LOG2E / math.sqrt(SB_HEAD_DIM))
    proj, vt = norm_matmul(x2d, norm_g, w_in_l.astype(jnp.bfloat16), seq=s,
                           vt_cols=(2 * SB_WIDTH, SB_WIDTH))
    o = sb_attention(proj.reshape(b, s, -1), vt)
    y = gate_out_proj(o.reshape(b * s, SB_WIDTH), proj, 3, x2d, w_out.astype(jnp.bfloat16))
    return y.reshape(b, s, d)


def _rope_tables(seq):
    half = MLA_ROPE_DIM // 2
    inv_freq = ROPE_THETA ** (-jnp.arange(half, dtype=jnp.float32) / half)
    ang = jnp.arange(seq, dtype=jnp.float32)[:, None] * inv_freq[None, :]
    cos, sin = jnp.cos(ang), jnp.sin(ang)
    zero = jnp.zeros((seq, LANES - MLA_ROPE_DIM), jnp.float32)
    return (jnp.concatenate([cos, cos, zero], axis=1),
            jnp.concatenate([-sin, sin, zero], axis=1))


def _rotate_half(pe):
    lane = lax.broadcasted_iota(jnp.int32, pe.shape, 1)
    half = MLA_ROPE_DIM // 2
    return jnp.where(lane < half, pltpu.roll(pe, LANES - half, axis=1), pltpu.roll(pe, half, axis=1))


def _mla_prep_kernel(qlat_ref, kvl_ref, cos_ref, sin_ref, qa_ref, kva_ref, gq_ref, gk_ref,
                     wuq_ref, wukv_ref, q_ref, k_ref, vt_ref):
    cos, sin = cos_ref[...], sin_ref[...]
    ql = qlat_ref[...].astype(jnp.float32)
    qn = (ql * _rms_scale(ql, MLA_Q_RANK) * qa_ref[...]).astype(jnp.bfloat16)
    kvl = kvl_ref[...].astype(jnp.float32)
    kv_lat = kvl[:, :MLA_KV_RANK]
    kvn = (kv_lat * _rms_scale(kv_lat, MLA_KV_RANK) * kva_ref[...]).astype(jnp.bfloat16)
    gq, gk = gq_ref[...], gk_ref[...]
    k_pe = kvl[:, MLA_KV_RANK:]
    k_pe_sq = jnp.sum(k_pe * k_pe, axis=-1, keepdims=True)
    k_pe_g = k_pe * gk[:, MLA_NOPE_DIM:]
    k_pe_rot = k_pe_g * cos + _rotate_half(k_pe_g) * sin
    for h in range(MLA_HEADS):
        c0 = h * MLA_QK_PAD
        qh = jnp.dot(qn, wuq_ref[:, c0:c0 + MLA_QK_PAD], preferred_element_type=jnp.float32)
        qh = qh * _rms_scale(qh, MLA_QK_DIM) * gq
        q_pe = qh[:, MLA_NOPE_DIM:]
        q_ref[:, c0:c0 + MLA_NOPE_DIM] = qh[:, :MLA_NOPE_DIM].astype(q_ref.dtype)
        q_ref[:, c0 + MLA_NOPE_DIM:c0 + MLA_QK_PAD] = (
            q_pe * cos + _rotate_half(q_pe) * sin).astype(q_ref.dtype)
        kvh = jnp.dot(kvn, wukv_ref[:, c0:c0 + MLA_QK_PAD], preferred_element_type=jnp.float32)
        k_nope = kvh[:, :MLA_NOPE_DIM]
        r = lax.rsqrt((jnp.sum(k_nope * k_nope, axis=-1, keepdims=True) + k_pe_sq)
                      * (1.0 / MLA_QK_DIM) + NORM_EPS)
        k_ref[:, c0:c0 + MLA_NOPE_DIM] = (k_nope * r * gk[:, :MLA_NOPE_DIM]).astype(k_ref.dtype)
        k_ref[:, c0 + MLA_NOPE_DIM:c0 + MLA_QK_PAD] = (k_pe_rot * r).astype(k_ref.dtype)
        vt_ref[0, h * MLA_V_DIM:(h + 1) * MLA_V_DIM, :] = kvh[:, MLA_NOPE_DIM:].T.astype(vt_ref.dtype)


def mla_prep(proj2d, seq, q_a_norm, kv_a_norm, q_head_norm, k_head_norm, wuq_pad, wukv, *, tm=512):
    n = proj2d.shape[0]
    cos, sin = _rope_tables(seq)
    pad = jnp.zeros((MLA_QK_PAD - MLA_QK_DIM,), jnp.float32)
    gq = jnp.concatenate([q_head_norm * (LOG2E / math.sqrt(MLA_QK_DIM)), pad]).reshape(1, MLA_QK_PAD)
    gk = jnp.concatenate([k_head_norm, pad]).reshape(1, MLA_QK_PAD)
    qlat_blk = MLA_WIDTH // MLA_Q_RANK
    tiles_per_seq = seq // tm
    row = lambda i: (i, 0)
    const = lambda i: (0, 0)
    pos = lambda i: (i % tiles_per_seq, 0)
    return pl.pallas_call(
        _mla_prep_kernel,
        grid=(n // tm,),
        in_specs=[pl.BlockSpec((tm, MLA_Q_RANK), lambda i: (i, qlat_blk)),
                  pl.BlockSpec((tm, MLA_Q_RANK), lambda i: (i, qlat_blk + 1)),
                  pl.BlockSpec((tm, LANES), pos),
                  pl.BlockSpec((tm, LANES), pos),
                  pl.BlockSpec((1, MLA_Q_RANK), const),
                  pl.BlockSpec((1, MLA_KV_RANK), const),
                  pl.BlockSpec((1, MLA_QK_PAD), const),
                  pl.BlockSpec((1, MLA_QK_PAD), const),
                  pl.BlockSpec((MLA_Q_RANK, MLA_HEADS * MLA_QK_PAD), const),
                  pl.BlockSpec((MLA_KV_RANK, MLA_HEADS * MLA_QK_PAD), const)],
        out_specs=[pl.BlockSpec((tm, MLA_HEADS * MLA_QK_PAD), row),
                   pl.BlockSpec((tm, MLA_HEADS * MLA_QK_PAD), row),
                   pl.BlockSpec((1, MLA_WIDTH, tm),
                                lambda i: (i // tiles_per_seq, 0, i % tiles_per_seq))],
        out_shape=[jax.ShapeDtypeStruct((n, MLA_HEADS * MLA_QK_PAD), jnp.bfloat16),
                   jax.ShapeDtypeStruct((n, MLA_HEADS * MLA_QK_PAD), jnp.bfloat16),
                   jax.ShapeDtypeStruct((n // seq, MLA_WIDTH, seq), jnp.bfloat16)],
        compiler_params=_cparams("arbitrary"),
        name="mla_prep",
    )(proj2d, proj2d, cos, sin, q_a_norm.reshape(1, -1), kv_a_norm.reshape(1, -1), gq, gk,
      wuq_pad, wukv)


def _causal_attn_kernel(q_ref, k_ref, vt_ref, o_ref, s_ref, p_ref, *, tile):
    seq = q_ref.shape[1]
    chains = range(2)
    group = 2 * tile
    causal = (lax.broadcasted_iota(jnp.int32, (tile, tile), 0)
              <= lax.broadcasted_iota(jnp.int32, (tile, tile), 1))

    def qk(q, s0):
        return lax.dot_general(k_ref[0, pl.ds(pl.multiple_of(s0, tile), tile), :], q,
                               (((1,), (1,)), ((), ())), preferred_element_type=jnp.float32)

    def pv(p, s0):
        return jnp.dot(vt_ref[0, :, pl.ds(pl.multiple_of(s0, tile), tile)], p,
                       preferred_element_type=jnp.float32)

    def softmax_step(s2, m, l, mask):
        if mask is not None:
            s2 = jnp.where(mask, s2, -jnp.inf)
        m_new = jnp.maximum(m, jnp.max(s2, axis=0, keepdims=True))
        p = jnp.exp2(s2 - m_new)
        alpha = jnp.exp2(m - m_new)
        return m_new, alpha * l + jnp.sum(p, axis=0, keepdims=True), alpha, p.astype(jnp.bfloat16)

    def start_group(t0):
        for c in chains:
            r0 = t0 + c * tile
            r0 = r0 if isinstance(r0, int) else pl.multiple_of(r0, tile)
            s_ref[c] = qk(q_ref[0, pl.ds(r0, tile), :], 0)

    for c in chains:
        p_ref[c] = jnp.zeros((tile, tile), p_ref.dtype)
    start_group(0)

    def q_group(i, _):
        t0 = pl.multiple_of(i * group, group)
        q = [q_ref[0, pl.ds(t0 + c * tile, tile), :] for c in chains]
        init = tuple((jnp.full((1, tile), -jnp.inf, jnp.float32), jnp.zeros((1, tile), jnp.float32),
                      jnp.zeros((vt_ref.shape[1], tile), jnp.float32),
                      jnp.ones((1, tile), jnp.float32)) for _ in chains)

        def first_half(j, state, s_b_chains, masks):
            b_prev = jnp.maximum(j - 1, 0) * group + tile
            pv_b = [pv(p_ref[c], b_prev) for c in chains]
            s_b = {c: qk(q[c], j * group + tile) for c in s_b_chains}
            out = []
            for c in chains:
                m, l, acc, alpha_b = state[c]
                acc = alpha_b * acc + jnp.where(j > 0, pv_b[c], 0.0)
                m, l, alpha_a, p_a = softmax_step(s_ref[c], m, l, masks[c])
                out.append((m, l, acc, alpha_a, p_a))
            return out, s_b

        def full_tiles(j, state):
            mid, s_b = first_half(j, state, chains, (None, None))
            pv_a = [pv(mid[c][4], j * group) for c in chains]
            for c in chains:
                s_ref[c] = qk(q[c], (j + 1) * group)
            out = []
            for c in chains:
                m, l, acc, alpha_a, _ = mid[c]
                acc = alpha_a * acc + pv_a[c]
                m, l, alpha_b, p_b = softmax_step(s_b[c], m, l, None)
                p_ref[c] = p_b
                out.append((m, l, acc, alpha_b))
            return tuple(out)

        state = lax.fori_loop(0, i, full_tiles, init)
        mid, s_b = first_half(i, state, (1,), (causal, None))
        start_group(jnp.minimum(t0 + group, seq - group))
        pv_a = [pv(mid[c][4], t0) for c in chains]
        m, l, acc, alpha_a, _ = mid[0]
        o_ref[0, pl.ds(t0, tile), :] = ((alpha_a * acc + pv_a[0]) / l).T.astype(o_ref.dtype)
        m, l, acc, alpha_a, _ = mid[1]
        acc = alpha_a * acc + pv_a[1]
        m, l, alpha_b, p_b = softmax_step(s_b[1], m, l, causal)
        acc = alpha_b * acc + pv(p_b, t0 + tile)
        o_ref[0, pl.ds(t0 + tile, tile), :] = (acc / l).T.astype(o_ref.dtype)
        return 0

    lax.fori_loop(0, seq // group, q_group, 0)


def mla_attention(q3d, k3d, vt3d, *, tile=256):
    b, s, _ = q3d.shape
    qk = lambda bi, h: (bi, 0, h)
    return pl.pallas_call(
        functools.partial(_causal_attn_kernel, tile=tile),
        grid=(b, MLA_HEADS),
        in_specs=[pl.BlockSpec((1, s, MLA_QK_PAD), qk),
                  pl.BlockSpec((1, s, MLA_QK_PAD), qk),
                  pl.BlockSpec((1, MLA_V_DIM, s), lambda bi, h: (bi, h, 0))],
        out_specs=pl.BlockSpec((1, s, MLA_V_DIM), qk),
        out_shape=jax.ShapeDtypeStruct((b, s, MLA_WIDTH), jnp.bfloat16),
        scratch_shapes=[pltpu.VMEM((2, tile, tile), jnp.float32),
                        pltpu.VMEM((2, tile, tile), jnp.bfloat16)],
        compiler_params=_cparams("arbitrary", "arbitrary"),
        name="mla_attention",
    )(q3d, k3d, vt3d)


def mla_layer(x, norm_g, w_in, q_a_norm, w_uq, kv_a_norm, w_ukv, q_head_norm, k_head_norm, w_out):
    b, s, d = x.shape
    x2d = x.reshape(b * s, d)
    c1 = MLA_Q_RANK
    c3 = c1 + MLA_KV_RANK + MLA_ROPE_DIM
    w_in_l = jnp.concatenate(
        [w_in[:, c3:], w_in[:, :c3], jnp.zeros((d, LANES - MLA_ROPE_DIM), w_in.dtype)], axis=1)
    wuq_pad = jnp.pad(w_uq.reshape(MLA_Q_RANK, MLA_HEADS, MLA_QK_DIM),
                      ((0, 0), (0, 0), (0, MLA_QK_PAD - MLA_QK_DIM))).reshape(MLA_Q_RANK, -1)
    proj, = norm_matmul(x2d, norm_g, w_in_l.astype(jnp.bfloat16))
    q, k, vt = mla_prep(proj, s, q_a_norm, kv_a_norm, q_head_norm, k_head_norm,
                        wuq_pad.astype(jnp.bfloat16), w_ukv.astype(jnp.bfloat16))
    o = mla_attention(q.reshape(b, s, -1), k.reshape(b, s, -1), vt)
    y = gate_out_proj(o.reshape(b * s, MLA_WIDTH), proj, 0, x2d, w_out.astype(jnp.bfloat16))
    return y.reshape(b, s, d)


def _halves_rms(xf, gain):
    lane = lax.broadcasted_iota(jnp.int32, xf.shape, 1)
    lo = lane < SWA_HEAD_DIM
    sq = xf * xf
    s_lo = jnp.sum(jnp.where(lo, sq, 0.0), axis=-1, keepdims=True)
    s_hi = jnp.sum(jnp.where(lo, 0.0, sq), axis=-1, keepdims=True)
    r = lax.rsqrt(jnp.where(lo, s_lo, s_hi) * (1.0 / SWA_HEAD_DIM) + NORM_EPS)
    return xf * r * gain


def _swa_attn_kernel(q_ref, k_ref, v_ref, gq_ref, gk_ref, slope_ref, sink_ref, o_ref,
                     kn_ref, vt_ref, bias_ref, *, tq):
    seq = q_ref.shape[1]
    ncol = q_ref.shape[2] // LANES
    halves = range(LANES // SWA_HEAD_DIM)
    tk = tq + SWA_WINDOW
    pair = pl.program_id(1)
    kn_ref[pl.ds(0, SWA_WINDOW), :] = jnp.zeros((SWA_WINDOW, LANES), kn_ref.dtype)
    kn_ref[pl.ds(SWA_WINDOW, seq), :] = _halves_rms(
        k_ref[0].astype(jnp.float32), gk_ref[...]).astype(kn_ref.dtype)
    vt_ref[:, pl.ds(0, SWA_WINDOW)] = jnp.zeros((LANES, SWA_WINDOW), vt_ref.dtype)
    vt_ref[:, pl.ds(SWA_WINDOW, seq)] = v_ref[0]
    lane = lax.broadcasted_iota(jnp.int32, (1, LANES), 1)
    in_half = [(lane >= h * SWA_HEAD_DIM) & (lane < (h + 1) * SWA_HEAD_DIM) for h in halves]
    key_row = lax.broadcasted_iota(jnp.int32, (tk, tq), 0)
    rel = lax.broadcasted_iota(jnp.int32, (tk, tq), 1) + SWA_WINDOW - key_row
    in_window = (rel >= 0) & (rel < SWA_WINDOW)
    rel_f = rel.astype(jnp.float32)
    for h in halves:
        for c in range(ncol):
            bias_ref[h, :, c * tq:(c + 1) * tq] = jnp.where(
                in_window, -slope_ref[pair, h, c] * rel_f, -jnp.inf)
    sink_row = [jnp.concatenate([jnp.full((1, tq), sink_ref[pair, h, c], jnp.float32)
                                 for c in range(ncol)], axis=1) for h in halves]
    before_start = jnp.concatenate([key_row < SWA_WINDOW] * ncol, axis=1)
    q_scale = 1.0 / math.sqrt(SWA_HEAD_DIM)

    def q_tile(i, first):
        t0 = pl.multiple_of(i * tq, tq)
        kb = kn_ref[pl.ds(t0, tk), :]
        qn = [(_halves_rms(q_ref[0, pl.ds(t0, tq), c * LANES:(c + 1) * LANES].astype(jnp.float32),
                           gq_ref[...]) * q_scale).astype(jnp.bfloat16) for c in range(ncol)]
        s_t = [lax.dot_general(
            kb, jnp.concatenate([jnp.where(in_half[h], qn[c], jnp.zeros_like(qn[c]))
                                 for c in range(ncol)], axis=0),
            (((1,), (1,)), ((), ())), preferred_element_type=jnp.float32) for h in halves]
        o_t = []
        for h in halves:
            s = s_t[h] + bias_ref[h]
            if first:
                s = jnp.where(before_start, -jnp.inf, s)
            m = jnp.maximum(jnp.max(s, axis=0, keepdims=True), sink_row[h])
            e = jnp.exp(s - m)
            denom = jnp.sum(e, axis=0, keepdims=True) + jnp.exp(sink_row[h] - m)
            pv = jnp.dot(vt_ref[h * SWA_HEAD_DIM:(h + 1) * SWA_HEAD_DIM, pl.ds(t0, tk)],
                         e.astype(jnp.bfloat16), preferred_element_type=jnp.float32)
            o_t.append(pv / denom)
        for c in range(ncol):
            o_col_t = jnp.concatenate([o_t[h][:, c * tq:(c + 1) * tq] for h in halves], axis=0)
            o_ref[0, pl.ds(t0, tq), c * LANES:(c + 1) * LANES] = o_col_t.T.astype(o_ref.dtype)

    q_tile(0, True)

    def q_tile_loop(i, _):
        q_tile(i, False)
        return 0

    lax.fori_loop(1, seq // tq, q_tile_loop, 0)


def swa_attention(proj3d, vt3d, gq2, gk2, slopes, sinks, *, tq=128):
    b, s, _ = proj3d.shape
    npair = SWA_KV_HEADS // 2
    qw = SWA_WIDTH // npair
    kcol0 = 2 * SWA_WIDTH // LANES
    smem = pl.BlockSpec(memory_space=pltpu.SMEM)
    return pl.pallas_call(
        functools.partial(_swa_attn_kernel, tq=tq),
        grid=(b, npair),
        in_specs=[pl.BlockSpec((1, s, qw), lambda bi, p: (bi, 0, p)),
                  pl.BlockSpec((1, s, LANES), lambda bi, p: (bi, 0, kcol0 + p)),
                  pl.BlockSpec((1, LANES, s), lambda bi, p: (bi, p, 0)),
                  pl.BlockSpec((1, LANES), lambda bi, p: (0, 0)),
                  pl.BlockSpec((1, LANES), lambda bi, p: (0, 0)),
                  smem, smem],
        out_specs=pl.BlockSpec((1, s, qw), lambda bi, p: (bi, 0, p)),
        out_shape=jax.ShapeDtypeStruct((b, s, SWA_WIDTH), jnp.bfloat16),
        scratch_shapes=[pltpu.VMEM((s + SWA_WINDOW, LANES), jnp.bfloat16),
                        pltpu.VMEM((LANES, s + SWA_WINDOW), jnp.bfloat16),
                        pltpu.VMEM((LANES // SWA_HEAD_DIM, tq + SWA_WINDOW, qw // LANES * tq),
                                   jnp.float32)],
        compiler_params=_cparams("arbitrary", "arbitrary"),
        name="swa_attention",
    )(proj3d, proj3d, vt3d, gq2, gk2, slopes, sinks)


def _swa_head_order():
    order = []
    for p in range(SWA_KV_HEADS // 2):
        for c in range(SWA_GROUP):
            order += [2 * p * SWA_GROUP + c, (2 * p + 1) * SWA_GROUP + c]
    return order


def swa_layer(x, norm_g, w_in, q_head_norm, k_head_norm, sinks, w_out):
    b, s, d = x.shape
    x2d = x.reshape(b * s, d)
    kv_w = SWA_KV_HEADS * SWA_HEAD_DIM
    c1, c2, c3 = SWA_WIDTH, SWA_WIDTH + kv_w, SWA_WIDTH + 2 * kv_w
    order = jnp.asarray(_swa_head_order())
    perm = (order[:, None] * SWA_HEAD_DIM + jnp.arange(SWA_HEAD_DIM)[None, :]).reshape(-1)
    w_in_l = jnp.concatenate([w_in[:, :c1][:, perm], w_in[:, c3:][:, perm], w_in[:, c1:c3]], axis=1)
    w_out_l = w_out[perm, :]
    slopes = 2.0 ** (-8.0 * jnp.arange(1, SWA_HEADS + 1, dtype=jnp.float32) / SWA_HEADS)
    by_pos = lambda t: t[order].reshape(SWA_KV_HEADS // 2, SWA_GROUP, 2).transpose(0, 2, 1)
    gq2 = jnp.tile(q_head_norm, 2).reshape(1, LANES)
    gk2 = jnp.tile(k_head_norm, 2).reshape(1, LANES)
    proj, vt = norm_matmul(x2d, norm_g, w_in_l.astype(jnp.bfloat16), seq=s,
                           vt_cols=(2 * SWA_WIDTH + kv_w, kv_w))
    o = swa_attention(proj.reshape(b, s, -1), vt, gq2, gk2, by_pos(slopes),
                      by_pos(sinks.astype(jnp.float32)))
    y = gate_out_proj(o.reshape(b * s, SWA_WIDTH), proj, 1, x2d, w_out_l.astype(jnp.bfloat16))
    return y.reshape(b, s, d)


def kernel(x, l0_norm, l0_w_in, l0_w_out, l1_norm, l1_w_in, l1_q_a_norm, l1_w_uq, l1_kv_a_norm,
           l1_w_ukv, l1_q_head_norm, l1_k_head_norm, l1_w_out, l2_norm, l2_w_in, l2_q_head_norm,
           l2_k_head_norm, l2_sinks, l2_w_out, l3_norm, l3_w_in, l3_w_out):
    x = stick_breaking_layer(x, l0_norm, l0_w_in, l0_w_out)
    x = mla_layer(x, l1_norm, l1_w_in, l1_q_a_norm, l1_w_uq, l1_kv_a_norm, l1_w_ukv,
                  l1_q_head_norm, l1_k_head_norm, l1_w_out)
    x = swa_layer(x, l2_norm, l2_w_in, l2_q_head_norm, l2_k_head_norm, l2_sinks, l2_w_out)
    x = stick_breaking_layer(x, l3_norm, l3_w_in, l3_w_out)
    return x
```

```python
import functools
import math

import jax
import jax.numpy as jnp
from jax import lax
from jax.experimental import pallas as pl
from jax.experimental.pallas import tpu as pltpu

NORM_EPS = 1e-6
LOG2E = 1.4426950408889634

LANES = 128
VMEM_LIMIT_BYTES = 56 * 1024 * 1024

SB_HEADS = 16
SB_HEAD_DIM = 64
SB_WIDTH = SB_HEADS * SB_HEAD_DIM

MLA_HEADS = 8
MLA_NOPE_DIM = 128
MLA_ROPE_DIM = 64
MLA_QK_DIM = MLA_NOPE_DIM + MLA_ROPE_DIM
MLA_QK_PAD = 256
MLA_V_DIM = 128
MLA_Q_RANK = 256
MLA_KV_RANK = 128
MLA_WIDTH = MLA_HEADS * MLA_V_DIM
ROPE_THETA = 10000.0

SWA_HEADS = 16
SWA_KV_HEADS = 4
SWA_GROUP = SWA_HEADS // SWA_KV_HEADS
SWA_HEAD_DIM = 64
SWA_WINDOW = 128
SWA_WIDTH = SWA_HEADS * SWA_HEAD_DIM

F32_EXP2_ZERO_BELOW = -160.0
MASKED_LOGIT = 1e30


def _cparams(*sem):
    return pltpu.CompilerParams(dimension_semantics=sem, vmem_limit_bytes=VMEM_LIMIT_BYTES)


def _rms_scale(xf, n):
    return lax.rsqrt(jnp.sum(xf * xf, axis=-1, keepdims=True) * (1.0 / n) + NORM_EPS)


def _halves_rms(xf, gain):
    lane = lax.broadcasted_iota(jnp.int32, xf.shape, 1)
    lo = lane < LANES // 2
    sq = xf * xf
    s_lo = jnp.sum(jnp.where(lo, sq, 0.0), axis=-1, keepdims=True)
    s_hi = jnp.sum(jnp.where(lo, 0.0, sq), axis=-1, keepdims=True)
    r = lax.rsqrt(jnp.where(lo, s_lo, s_hi) * (2.0 / LANES) + NORM_EPS)
    return xf * r * gain


def _norm_matmul_kernel(x_ref, g_ref, w_ref, *refs, n_chunk, vt_cols, head_cols):
    refs = list(refs)
    hg_ref = refs.pop(0) if head_cols else None
    o_ref = refs.pop(0)
    vt_ref = refs.pop(0) if vt_cols is not None else None
    xf = x_ref[...]
    xn = (xf * _rms_scale(xf, xf.shape[-1]) * g_ref[...]).astype(jnp.bfloat16)
    n_out = o_ref.shape[-1]
    for c in range(0, n_out, n_chunk):
        acc = jnp.dot(xn, w_ref[:, c:c + n_chunk], preferred_element_type=jnp.float32)
        if any(lo < c + n_chunk and c < hi for lo, hi in head_cols):
            acc = jnp.concatenate(
                [_halves_rms(acc[:, j:j + LANES], hg_ref[:, c + j:c + j + LANES])
                 if any(lo <= c + j < hi for lo, hi in head_cols) else acc[:, j:j + LANES]
                 for j in range(0, n_chunk, LANES)], axis=1)
        o_ref[:, c:c + n_chunk] = acc.astype(o_ref.dtype)
        if vt_cols is not None:
            lo, hi = max(c, vt_cols[0]), min(c + n_chunk, vt_cols[0] + vt_cols[1])
            if lo < hi:
                vt_ref[0, lo - vt_cols[0]:hi - vt_cols[0], :] = (
                    acc[:, lo - c:hi - c].T.astype(vt_ref.dtype))


def norm_matmul(x2d, g, w_bf16, *, seq=None, vt_cols=None, head_gain=None, head_cols=(),
                tm=512, n_chunk=512):
    n, d = x2d.shape
    n_out = w_bf16.shape[1]
    in_specs = [pl.BlockSpec((tm, d), lambda i: (i, 0)),
                pl.BlockSpec((1, d), lambda i: (0, 0)),
                pl.BlockSpec((d, n_out), lambda i: (0, 0))]
    args = [x2d, g.reshape(1, d), w_bf16]
    if head_cols:
        in_specs.append(pl.BlockSpec((1, n_out), lambda i: (0, 0)))
        args.append(head_gain)
    out_specs = [pl.BlockSpec((tm, n_out), lambda i: (i, 0))]
    out_shape = [jax.ShapeDtypeStruct((n, n_out), jnp.bfloat16)]
    if vt_cols is not None:
        tiles_per_seq = seq // tm
        out_specs.append(pl.BlockSpec((1, vt_cols[1], tm),
                                      lambda i: (i // tiles_per_seq, 0, i % tiles_per_seq)))
        out_shape.append(jax.ShapeDtypeStruct((n // seq, vt_cols[1], seq), jnp.bfloat16))
    return pl.pallas_call(
        functools.partial(_norm_matmul_kernel, n_chunk=n_chunk, vt_cols=vt_cols,
                          head_cols=tuple(head_cols)),
        grid=(n // tm,),
        in_specs=in_specs,
        out_specs=out_specs,
        out_shape=out_shape,
        compiler_params=_cparams("arbitrary"),
        name="norm_matmul",
    )(*args)


def _gate_out_proj_kernel(o_ref, gate_ref, x_ref, w_ref, y_ref):
    gate = gate_ref[...].astype(jnp.float32)
    og = (o_ref[...].astype(jnp.float32) * (gate * jax.nn.sigmoid(gate))).astype(jnp.bfloat16)
    y_ref[...] = x_ref[...] + jnp.dot(og, w_ref[...], preferred_element_type=jnp.float32)


def gate_out_proj(o2d, proj2d, gate_block, x2d, w_bf16, *, tm=512):
    n, width = o2d.shape
    d = x2d.shape[1]
    return pl.pallas_call(
        _gate_out_proj_kernel,
        grid=(n // tm,),
        in_specs=[pl.BlockSpec((tm, width), lambda i: (i, 0)),
                  pl.BlockSpec((tm, width), lambda i: (i, gate_block)),
                  pl.BlockSpec((tm, d), lambda i: (i, 0)),
                  pl.BlockSpec((width, d), lambda i: (0, 0))],
        out_specs=pl.BlockSpec((tm, d), lambda i: (i, 0)),
        out_shape=jax.ShapeDtypeStruct((n, d), jnp.float32),
        compiler_params=_cparams("arbitrary"),
        name="gate_out_proj",
    )(o2d, proj2d, x2d, w_bf16)


def _sb_scores(k_blk, qm):
    return [lax.dot_general(k_blk, q_h, (((1,), (1,)), ((), ())), preferred_element_type=jnp.float32)
            for q_h in qm]


def _sb_front(z, mask, carry, tri):
    z2 = z
    if mask is not None:
        z2 = jnp.where(mask, z2, -MASKED_LOGIT)
    soft = jnp.log2(1.0 + jnp.exp2(-jnp.abs(z2)))
    ls2 = jnp.minimum(z2, 0.0) - soft
    lf2 = ls2 - z2
    later = jnp.dot(tri, lf2.astype(jnp.bfloat16), preferred_element_type=jnp.float32)
    return (ls2 if carry is None else ls2 + carry), later, jnp.sum(lf2, axis=0, keepdims=True)


def _sb_back(arg, later, vt_blk):
    return jnp.dot(vt_blk, jnp.exp2(arg + later).astype(jnp.bfloat16),
                   preferred_element_type=jnp.float32)


def _sb_attn_kernel(q_ref, k_ref, vt_ref, o_ref, arg_ref, later_ref, *, tile, head_dim):
    seq = q_ref.shape[1]
    n_tiles = seq // tile
    heads = range(LANES // head_dim)
    lane = lax.broadcasted_iota(jnp.int32, (1, LANES), 1)
    in_head = [(lane >= h * head_dim) & (lane < (h + 1) * head_dim) for h in heads]
    earlier = (lax.broadcasted_iota(jnp.int32, (tile, tile), 0)
               < lax.broadcasted_iota(jnp.int32, (tile, tile), 1))
    tri = earlier.astype(jnp.bfloat16)

    def rows(s0):
        return pl.ds(s0 if isinstance(s0, int) else pl.multiple_of(s0, tile), tile)

    def masked_q(t0):
        q = q_ref[0, rows(t0), :]
        return [jnp.where(m, q, jnp.zeros_like(q)) for m in in_head]

    def vt_tile(s0, h):
        return vt_ref[0, h * head_dim:(h + 1) * head_dim, rows(s0)]

    def blocks(z_kinds, masks, key_starts, acc):
        acc = None if acc is None else list(acc)
        carry = [None for _ in heads]
        for kind in range(2 if key_starts is not None else 0):
            for h in heads:
                b = 2 * kind + h
                acc[h] = acc[h] + _sb_back(arg_ref[b], later_ref[b], vt_tile(key_starts[kind], h))
        for kind in range(len(z_kinds) if z_kinds is not None else 0):
            for h in heads:
                b = 2 * kind + h
                arg, later, colsum = _sb_front(z_kinds[kind][h], masks[kind], carry[h], tri)
                arg_ref[b] = arg
                later_ref[b] = later
                carry[h] = colsum if carry[h] is None else carry[h] + colsum
        return acc, carry

    def write_out(t0, acc):
        o_ref[0, rows(t0), :] = jnp.concatenate(acc, axis=0).T.astype(o_ref.dtype)

    zero_acc = tuple(jnp.zeros((head_dim, tile), jnp.float32) for _ in heads)

    blocks([_sb_scores(k_ref[0, rows(0), :], masked_q(0))], [earlier], None, None)
    for h in heads:
        arg_ref[2 + h] = jnp.full((tile, tile), -MASKED_LOGIT, jnp.float32)
        later_ref[2 + h] = jnp.zeros((tile, tile), jnp.float32)

    def q_tile(i, acc_swept):
        t0 = i * tile
        qm = masked_q(t0)
        z_kinds = [_sb_scores(k_ref[0, rows(t0), :], qm), _sb_scores(k_ref[0, rows(t0 - tile), :], qm)]
        acc_prev, carry = blocks(z_kinds, [earlier, None],
                                 [t0 - tile, jnp.maximum(t0 - 2 * tile, 0)], acc_swept)
        write_out(t0 - tile, acc_prev)

        def alive(carry):
            return jnp.max(functools.reduce(jnp.maximum, carry)) >= F32_EXP2_ZERO_BELOW

        def cond(state):
            jj, live, _, _ = state
            return (jj <= i) & live

        def sweep(state):
            jj, _, acc, carry = state
            s0 = t0 - jj * tile
            z = _sb_scores(k_ref[0, rows(s0), :], qm)
            acc, carry = list(acc), list(carry)
            for h in heads:
                arg, later, colsum = _sb_front(z[h], None, carry[h], tri)
                acc[h] = acc[h] + _sb_back(arg, later, vt_tile(s0, h))
                carry[h] = carry[h] + colsum
            return jj + 1, alive(carry), tuple(acc), tuple(carry)

        _, _, acc, _ = lax.while_loop(cond, sweep, (jnp.int32(2), alive(carry), zero_acc,
                                                    tuple(carry)))
        return acc

    acc_swept = lax.fori_loop(1, n_tiles, q_tile, zero_acc)
    last = (n_tiles - 1) * tile
    write_out(last, blocks(None, None, [last, max(last - tile, 0)], acc_swept)[0])


def sb_attention(proj3d, vt3d, *, tile=256):
    b, s, _ = proj3d.shape
    cols = SB_WIDTH // LANES
    blk = lambda off: pl.BlockSpec((1, s, LANES), lambda bi, p: (bi, 0, off + p))
    return pl.pallas_call(
        functools.partial(_sb_attn_kernel, tile=tile, head_dim=SB_HEAD_DIM),
        grid=(b, cols),
        in_specs=[blk(0), blk(cols), pl.BlockSpec((1, LANES, s), lambda bi, p: (bi, p, 0))],
        out_specs=pl.BlockSpec((1, s, LANES), lambda bi, p: (bi, 0, p)),
        out_shape=jax.ShapeDtypeStruct((b, s, SB_WIDTH), jnp.bfloat16),
        scratch_shapes=[pltpu.VMEM((2 * LANES // SB_HEAD_DIM, tile, tile), jnp.float32),
                        pltpu.VMEM((2 * LANES // SB_HEAD_DIM, tile, tile), jnp.float32)],
        compiler_params=_cparams("arbitrary", "arbitrary"),
        name="sb_attention",
    )(proj3d, proj3d, vt3d)


def stick_breaking_layer(x, norm_g, w_in, w_out):
    b, s, d = x.shape
    x2d = x.reshape(b * s, d)
    w_in_l = w_in.at[:, :SB_WIDTH].multiply(LOG2E / math.sqrt(SB_HEAD_DIM))
    proj, vt = norm_matmul(x2d, norm_g, w_in_l.astype(jnp.bfloat16), seq=s,
                           vt_cols=(2 * SB_WIDTH, SB_WIDTH))
    o = sb_attention(proj.reshape(b, s, -1), vt)
    y = gate_out_proj(o.reshape(b * s, SB_WIDTH), proj, 3, x2d, w_out.astype(jnp.bfloat16))
    return y.reshape(b, s, d)


def _rope_tables(seq):
    half = MLA_ROPE_DIM // 2
    inv_freq = ROPE_THETA ** (-jnp.arange(half, dtype=jnp.float32) / half)
    ang = jnp.arange(seq, dtype=jnp.float32)[:, None] * inv_freq[None, :]
    cos, sin = jnp.cos(ang), jnp.sin(ang)
    zero = jnp.zeros((seq, LANES - MLA_ROPE_DIM), jnp.float32)
    return (jnp.concatenate([cos, cos, zero], axis=1),
            jnp.concatenate([-sin, sin, zero], axis=1))


def _rotate_half(pe):
    lane = lax.broadcasted_iota(jnp.int32, pe.shape, 1)
    half = MLA_ROPE_DIM // 2
    return jnp.where(lane < half, pltpu.roll(pe, LANES - half, axis=1), pltpu.roll(pe, half, axis=1))


def _mla_prep_kernel(qlat_ref, kvl_ref, cos_ref, sin_ref, qa_ref, kva_ref, gq_ref, gk_ref,
                     wuq_ref, wukv_ref, q_ref, k_ref, vt_ref):
    cos, sin = cos_ref[...], sin_ref[...]
    ql = qlat_ref[...].astype(jnp.float32)
    qn = (ql * _rms_scale(ql, MLA_Q_RANK) * qa_ref[...]).astype(jnp.bfloat16)
    kvl = kvl_ref[...].astype(jnp.float32)
    kv_lat = kvl[:, :MLA_KV_RANK]
    kvn = (kv_lat * _rms_scale(kv_lat, MLA_KV_RANK) * kva_ref[...]).astype(jnp.bfloat16)
    gq, gk = gq_ref[...], gk_ref[...]
    k_pe = kvl[:, MLA_KV_RANK:]
    k_pe_sq = jnp.sum(k_pe * k_pe, axis=-1, keepdims=True)
    k_pe_g = k_pe * gk[:, MLA_NOPE_DIM:]
    k_pe_rot = k_pe_g * cos + _rotate_half(k_pe_g) * sin
    for h in range(MLA_HEADS):
        c0 = h * MLA_QK_PAD
        qh = jnp.dot(qn, wuq_ref[:, c0:c0 + MLA_QK_PAD], preferred_element_type=jnp.float32)
        q_pe = qh[:, MLA_NOPE_DIM:]
        q_nope = qh[:, :MLA_NOPE_DIM]
        q_ss = jnp.sum(q_nope * q_nope + 0.5 * (q_pe * q_pe), axis=-1, keepdims=True)
        r = lax.rsqrt(q_ss * (1.0 / MLA_QK_DIM) + NORM_EPS)
        q_ref[:, c0:c0 + MLA_NOPE_DIM] = (q_nope * r * gq[:, :MLA_NOPE_DIM]).astype(q_ref.dtype)
        q_pe = q_pe * r * gq[:, MLA_NOPE_DIM:]
        q_ref[:, c0 + MLA_NOPE_DIM:c0 + MLA_QK_PAD] = (
            q_pe * cos + pltpu.roll(q_pe, MLA_ROPE_DIM // 2, axis=1) * sin).astype(q_ref.dtype)
        kvh = jnp.dot(kvn, wukv_ref[:, c0:c0 + MLA_QK_PAD], preferred_element_type=jnp.float32)
        k_nope = kvh[:, :MLA_NOPE_DIM]
        r = lax.rsqrt((jnp.sum(k_nope * k_nope, axis=-1, keepdims=True) + k_pe_sq)
                      * (1.0 / MLA_QK_DIM) + NORM_EPS)
        k_ref[:, c0:c0 + MLA_NOPE_DIM] = (k_nope * r * gk[:, :MLA_NOPE_DIM]).astype(k_ref.dtype)
        k_ref[:, c0 + MLA_NOPE_DIM:c0 + MLA_QK_PAD] = (k_pe_rot * r).astype(k_ref.dtype)
        vt_ref[0, h * MLA_V_DIM:(h + 1) * MLA_V_DIM, :] = kvh[:, MLA_NOPE_DIM:].T.astype(vt_ref.dtype)


def mla_prep(proj2d, seq, q_a_norm, kv_a_norm, q_head_norm, k_head_norm, wuq_pad, wukv, *, tm=512):
    n = proj2d.shape[0]
    cos, sin = _rope_tables(seq)
    pad = jnp.zeros((MLA_QK_PAD - MLA_QK_DIM,), jnp.float32)
    gq = q_head_norm * (LOG2E / math.sqrt(MLA_QK_DIM))
    gq = jnp.concatenate([gq, gq[MLA_NOPE_DIM:]]).reshape(1, MLA_QK_PAD)
    gk = jnp.concatenate([k_head_norm, pad]).reshape(1, MLA_QK_PAD)
    qlat_blk = MLA_WIDTH // MLA_Q_RANK
    tiles_per_seq = seq // tm
    row = lambda i: (i, 0)
    const = lambda i: (0, 0)
    pos = lambda i: (i % tiles_per_seq, 0)
    return pl.pallas_call(
        _mla_prep_kernel,
        grid=(n // tm,),
        in_specs=[pl.BlockSpec((tm, MLA_Q_RANK), lambda i: (i, qlat_blk)),
                  pl.BlockSpec((tm, MLA_Q_RANK), lambda i: (i, qlat_blk + 1)),
                  pl.BlockSpec((tm, LANES), pos),
                  pl.BlockSpec((tm, LANES), pos),
                  pl.BlockSpec((1, MLA_Q_RANK), const),
                  pl.BlockSpec((1, MLA_KV_RANK), const),
                  pl.BlockSpec((1, MLA_QK_PAD), const),
                  pl.BlockSpec((1, MLA_QK_PAD), const),
                  pl.BlockSpec((MLA_Q_RANK, MLA_HEADS * MLA_QK_PAD), const),
                  pl.BlockSpec((MLA_KV_RANK, MLA_HEADS * MLA_QK_PAD), const)],
        out_specs=[pl.BlockSpec((tm, MLA_HEADS * MLA_QK_PAD), row),
                   pl.BlockSpec((tm, MLA_HEADS * MLA_QK_PAD), row),
                   pl.BlockSpec((1, MLA_WIDTH, tm),
                                lambda i: (i // tiles_per_seq, 0, i % tiles_per_seq))],
        out_shape=[jax.ShapeDtypeStruct((n, MLA_HEADS * MLA_QK_PAD), jnp.bfloat16),
                   jax.ShapeDtypeStruct((n, MLA_HEADS * MLA_QK_PAD), jnp.bfloat16),
                   jax.ShapeDtypeStruct((n // seq, MLA_WIDTH, seq), jnp.bfloat16)],
        compiler_params=_cparams("arbitrary"),
        name="mla_prep",
    )(proj2d, proj2d, cos, sin, q_a_norm.reshape(1, -1), kv_a_norm.reshape(1, -1), gq, gk,
      wuq_pad, wukv)


def _causal_attn_kernel(q_ref, k_ref, vt_ref, o_ref, s_ref, p_ref, *, tile):
    seq = q_ref.shape[1]
    chains = range(2)
    group = 2 * tile
    causal = (lax.broadcasted_iota(jnp.int32, (tile, tile), 0)
              <= lax.broadcasted_iota(jnp.int32, (tile, tile), 1))

    def qk(q, s0):
        return lax.dot_general(k_ref[0, pl.ds(pl.multiple_of(s0, tile), tile), :], q,
                               (((1,), (1,)), ((), ())), preferred_element_type=jnp.float32)

    def pv(p, s0):
        return jnp.dot(vt_ref[0, :, pl.ds(pl.multiple_of(s0, tile), tile)], p,
                       preferred_element_type=jnp.float32)

    def softmax_step(s2, m, l, mask):
        if mask is not None:
            s2 = jnp.where(mask, s2, -jnp.inf)
        m_new = jnp.maximum(m, jnp.max(s2, axis=0, keepdims=True))
        p = jnp.exp2(s2 - m_new)
        alpha = jnp.exp2(m - m_new)
        return m_new, alpha * l + jnp.sum(p, axis=0, keepdims=True), alpha, p.astype(jnp.bfloat16)

    def start_group(t0):
        for c in chains:
            r0 = t0 + c * tile
            r0 = r0 if isinstance(r0, int) else pl.multiple_of(r0, tile)
            s_ref[c] = qk(q_ref[0, pl.ds(r0, tile), :], 0)

    for c in chains:
        p_ref[c] = jnp.zeros((tile, tile), p_ref.dtype)
    start_group(0)

    def q_group(i, _):
        t0 = pl.multiple_of(i * group, group)
        q = [q_ref[0, pl.ds(t0 + c * tile, tile), :] for c in chains]
        init = tuple((jnp.full((1, tile), -jnp.inf, jnp.float32), jnp.zeros((1, tile), jnp.float32),
                      jnp.zeros((vt_ref.shape[1], tile), jnp.float32),
                      jnp.ones((1, tile), jnp.float32)) for _ in chains)

        def first_half(j, state, s_b_chains, masks):
            b_prev = jnp.maximum(j - 1, 0) * group + tile
            pv_b = [pv(p_ref[c], b_prev) for c in chains]
            s_b = {c: qk(q[c], j * group + tile) for c in s_b_chains}
            out = []
            for c in chains:
                m, l, acc, alpha_b = state[c]
                acc = alpha_b * acc + jnp.where(j > 0, pv_b[c], 0.0)
                m, l, alpha_a, p_a = softmax_step(s_ref[c], m, l, masks[c])
                out.append((m, l, acc, alpha_a, p_a))
            return out, s_b

        def full_tiles(j, state):
            mid, s_b = first_half(j, state, chains, (None, None))
            pv_a = [pv(mid[c][4], j * group) for c in chains]
            for c in chains:
                s_ref[c] = qk(q[c], (j + 1) * group)
            out = []
            for c in chains:
                m, l, acc, alpha_a, _ = mid[c]
                acc = alpha_a * acc + pv_a[c]
                m, l, alpha_b, p_b = softmax_step(s_b[c], m, l, None)
                p_ref[c] = p_b
                out.append((m, l, acc, alpha_b))
            return tuple(out)

        state = lax.fori_loop(0, i, full_tiles, init)
        mid, s_b = first_half(i, state, (1,), (causal, None))
        start_group(jnp.minimum(t0 + group, seq - group))
        pv_a = [pv(mid[c][4], t0) for c in chains]
        m, l, acc, alpha_a, _ = mid[0]
        o_ref[0, pl.ds(t0, tile), :] = ((alpha_a * acc + pv_a[0]) / l).T.astype(o_ref.dtype)
        m, l, acc, alpha_a, _ = mid[1]
        acc = alpha_a * acc + pv_a[1]
        m, l, alpha_b, p_b = softmax_step(s_b[1], m, l, causal)
        acc = alpha_b * acc + pv(p_b, t0 + tile)
        o_ref[0, pl.ds(t0 + tile, tile), :] = (acc / l).T.astype(o_ref.dtype)
        return 0

    lax.fori_loop(0, seq // group, q_group, 0)


def mla_attention(q3d, k3d, vt3d, *, tile=256):
    b, s, _ = q3d.shape
    qk = lambda bi, h: (bi, 0, h)
    return pl.pallas_call(
        functools.partial(_causal_attn_kernel, tile=tile),
        grid=(b, MLA_HEADS),
        in_specs=[pl.BlockSpec((1, s, MLA_QK_PAD), qk),
                  pl.BlockSpec((1, s, MLA_QK_PAD), qk),
                  pl.BlockSpec((1, MLA_V_DIM, s), lambda bi, h: (bi, h, 0))],
        out_specs=pl.BlockSpec((1, s, MLA_V_DIM), qk),
        out_shape=jax.ShapeDtypeStruct((b, s, MLA_WIDTH), jnp.bfloat16),
        scratch_shapes=[pltpu.VMEM((2, tile, tile), jnp.float32),
                        pltpu.VMEM((2, tile, tile), jnp.bfloat16)],
        compiler_params=_cparams("arbitrary", "arbitrary"),
        name="mla_attention",
    )(q3d, k3d, vt3d)


def mla_layer(x, norm_g, w_in, q_a_norm, w_uq, kv_a_norm, w_ukv, q_head_norm, k_head_norm, w_out):
    b, s, d = x.shape
    x2d = x.reshape(b * s, d)
    c1 = MLA_Q_RANK
    c3 = c1 + MLA_KV_RANK + MLA_ROPE_DIM
    w_in_l = jnp.concatenate(
        [w_in[:, c3:], w_in[:, :c3], jnp.zeros((d, LANES - MLA_ROPE_DIM), w_in.dtype)], axis=1)
    wuq3 = w_uq.reshape(MLA_Q_RANK, MLA_HEADS, MLA_QK_DIM)
    wuq_pad = jnp.concatenate([wuq3, wuq3[:, :, MLA_NOPE_DIM:]], axis=2).reshape(MLA_Q_RANK, -1)
    proj, = norm_matmul(x2d, norm_g, w_in_l.astype(jnp.bfloat16))
    q, k, vt = mla_prep(proj, s, q_a_norm, kv_a_norm, q_head_norm, k_head_norm,
                        wuq_pad.astype(jnp.bfloat16), w_ukv.astype(jnp.bfloat16))
    o = mla_attention(q.reshape(b, s, -1), k.reshape(b, s, -1), vt)
    y = gate_out_proj(o.reshape(b * s, MLA_WIDTH), proj, 0, x2d, w_out.astype(jnp.bfloat16))
    return y.reshape(b, s, d)


def _swa_attn_kernel(q_ref, k_ref, v_ref, slope_ref, sink_ref, o_ref, kn_ref, vt_ref, bias_ref,
                     s_ref, *, tq):
    seq = q_ref.shape[1]
    ncol = q_ref.shape[2] // LANES
    halves = range(LANES // SWA_HEAD_DIM)
    col_pairs = range(ncol // 2)
    tk = tq + SWA_WINDOW
    pair = pl.program_id(1)
    kn_ref[pl.ds(0, SWA_WINDOW), :] = jnp.zeros((SWA_WINDOW, LANES), kn_ref.dtype)
    kn_ref[pl.ds(SWA_WINDOW, seq), :] = k_ref[0]
    vt_ref[:, pl.ds(0, SWA_WINDOW)] = jnp.zeros((LANES, SWA_WINDOW), vt_ref.dtype)
    vt_ref[:, pl.ds(SWA_WINDOW, seq)] = v_ref[0]
    lane = lax.broadcasted_iota(jnp.int32, (1, LANES), 1)
    in_half = [(lane >= h * SWA_HEAD_DIM) & (lane < (h + 1) * SWA_HEAD_DIM) for h in halves]
    key_row = lax.broadcasted_iota(jnp.int32, (tk, tq), 0)
    rel = lax.broadcasted_iota(jnp.int32, (tk, tq), 1) + SWA_WINDOW - key_row
    in_window = (rel >= 0) & (rel < SWA_WINDOW)
    rel_f = rel.astype(jnp.float32)
    for h in halves:
        for c in range(ncol):
            bias_ref[h, :, c * tq:(c + 1) * tq] = jnp.where(
                in_window, -slope_ref[pair, h, c] * rel_f, -jnp.inf)
    sink_row = [[jnp.concatenate([jnp.full((1, tq), sink_ref[pair, h, 2 * cp + cc], jnp.float32)
                                  for cc in range(2)], axis=1) for cp in col_pairs] for h in halves]
    before_start = jnp.concatenate([key_row < SWA_WINDOW] * 2, axis=1)

    chains = [(h, cp) for h in halves for cp in col_pairs]

    def scores(t0):
        kb = kn_ref[pl.ds(t0, tk), :]
        qn = [q_ref[0, pl.ds(t0, tq), c * LANES:(c + 1) * LANES] for c in range(ncol)]
        return [lax.dot_general(
            kb, jnp.concatenate([jnp.where(in_half[h], qn[2 * cp + cc], jnp.zeros_like(qn[0]))
                                 for cc in range(2)], axis=0),
            (((1,), (1,)), ((), ())), preferred_element_type=jnp.float32) for h, cp in chains]

    for idx, s0 in enumerate(scores(0)):
        s_ref[idx] = s0

    def q_tile(i, first):
        t0 = pl.multiple_of(i * tq, tq)
        s_next = scores(pl.multiple_of(jnp.minimum(t0 + tq, seq - tq), tq))
        o_t = {}
        for idx, (h, cp) in enumerate(chains):
            s = s_ref[idx] + bias_ref[h, :, 2 * cp * tq:2 * (cp + 1) * tq]
            s_ref[idx] = s_next[idx]
            if first:
                s = jnp.where(before_start, -jnp.inf, s)
            m = jnp.maximum(jnp.max(s, axis=0, keepdims=True), sink_row[h][cp])
            e = jnp.exp(s - m)
            denom = jnp.sum(e, axis=0, keepdims=True) + jnp.exp(sink_row[h][cp] - m)
            pv = jnp.dot(vt_ref[h * SWA_HEAD_DIM:(h + 1) * SWA_HEAD_DIM, pl.ds(t0, tk)],
                         e.astype(jnp.bfloat16), preferred_element_type=jnp.float32)
            o_t[h, cp] = pv / denom
        for c in range(ncol):
            cp, cc = divmod(c, 2)
            o_col_t = jnp.concatenate([o_t[h, cp][:, cc * tq:(cc + 1) * tq] for h in halves], axis=0)
            o_ref[0, pl.ds(t0, tq), c * LANES:(c + 1) * LANES] = o_col_t.T.astype(o_ref.dtype)

    q_tile(0, True)

    def q_tile_loop(i, _):
        q_tile(i, False)
        return 0

    lax.fori_loop(1, seq // tq, q_tile_loop, 0)


def swa_attention(proj3d, vt3d, slopes, sinks, *, tq=128):
    b, s, _ = proj3d.shape
    npair = SWA_KV_HEADS // 2
    qw = SWA_WIDTH // npair
    kcol0 = 2 * SWA_WIDTH // LANES
    smem = pl.BlockSpec(memory_space=pltpu.SMEM)
    return pl.pallas_call(
        functools.partial(_swa_attn_kernel, tq=tq),
        grid=(b, npair),
        in_specs=[pl.BlockSpec((1, s, qw), lambda bi, p: (bi, 0, p)),
                  pl.BlockSpec((1, s, LANES), lambda bi, p: (bi, 0, kcol0 + p)),
                  pl.BlockSpec((1, LANES, s), lambda bi, p: (bi, p, 0)),
                  smem, smem],
        out_specs=pl.BlockSpec((1, s, qw), lambda bi, p: (bi, 0, p)),
        out_shape=jax.ShapeDtypeStruct((b, s, SWA_WIDTH), jnp.bfloat16),
        scratch_shapes=[pltpu.VMEM((s + SWA_WINDOW, LANES), jnp.bfloat16),
                        pltpu.VMEM((LANES, s + SWA_WINDOW), jnp.bfloat16),
                        pltpu.VMEM((LANES // SWA_HEAD_DIM, tq + SWA_WINDOW, qw // LANES * tq),
                                   jnp.float32),
                        pltpu.VMEM((qw // LANES, tq + SWA_WINDOW, 2 * tq), jnp.float32)],
        compiler_params=_cparams("arbitrary", "arbitrary"),
        name="swa_attention",
    )(proj3d, proj3d, vt3d, slopes, sinks)


def _swa_head_order():
    order = []
    for p in range(SWA_KV_HEADS // 2):
        for c in range(SWA_GROUP):
            order += [2 * p * SWA_GROUP + c, (2 * p + 1) * SWA_GROUP + c]
    return order


def swa_layer(x, norm_g, w_in, q_head_norm, k_head_norm, sinks, w_out):
    b, s, d = x.shape
    x2d = x.reshape(b * s, d)
    kv_w = SWA_KV_HEADS * SWA_HEAD_DIM
    c1, c2, c3 = SWA_WIDTH, SWA_WIDTH + kv_w, SWA_WIDTH + 2 * kv_w
    order = jnp.asarray(_swa_head_order())
    perm = (order[:, None] * SWA_HEAD_DIM + jnp.arange(SWA_HEAD_DIM)[None, :]).reshape(-1)
    w_in_l = jnp.concatenate([w_in[:, :c1][:, perm], w_in[:, c3:][:, perm], w_in[:, c1:c3]], axis=1)
    w_out_l = w_out[perm, :]
    slopes = 2.0 ** (-8.0 * jnp.arange(1, SWA_HEADS + 1, dtype=jnp.float32) / SWA_HEADS)
    by_pos = lambda t: t[order].reshape(SWA_KV_HEADS // 2, SWA_GROUP, 2).transpose(0, 2, 1)
    q_cols, k_cols = (0, SWA_WIDTH), (2 * SWA_WIDTH, 2 * SWA_WIDTH + kv_w)
    head_gain = jnp.zeros((1, w_in_l.shape[1]), jnp.float32)
    head_gain = head_gain.at[0, q_cols[0]:q_cols[1]].set(
        jnp.tile(q_head_norm, SWA_HEADS) / math.sqrt(SWA_HEAD_DIM))
    head_gain = head_gain.at[0, k_cols[0]:k_cols[1]].set(jnp.tile(k_head_norm, SWA_KV_HEADS))
    proj, vt = norm_matmul(x2d, norm_g, w_in_l.astype(jnp.bfloat16), seq=s,
                           vt_cols=(2 * SWA_WIDTH + kv_w, kv_w),
                           head_gain=head_gain, head_cols=(q_cols, k_cols))
    o = swa_attention(proj.reshape(b, s, -1), vt, by_pos(slopes), by_pos(sinks.astype(jnp.float32)))
    y = gate_out_proj(o.reshape(b * s, SWA_WIDTH), proj, 1, x2d, w_out_l.astype(jnp.bfloat16))
    return y.reshape(b, s, d)


def kernel(x, l0_norm, l0_w_in, l0_w_out, l1_norm, l1_w_in, l1_q_a_norm, l1_w_uq, l1_kv_a_norm,
           l1_w_ukv, l1_q_head_norm, l1_k_head_norm, l1_w_out, l2_norm, l2_w_in, l2_q_head_norm,
           l2_k_head_norm, l2_sinks, l2_w_out, l3_norm, l3_w_in, l3_w_out):
    x = stick_breaking_layer(x, l0_norm, l0_w_in, l0_w_out)
    x = mla_layer(x, l1_norm, l1_w_in, l1_q_a_norm, l1_w_uq, l1_kv_a_norm, l1_w_ukv,
                  l1_q_head_norm, l1_k_head_norm, l1_w_out)
    x = swa_layer(x, l2_norm, l2_w_in, l2_q_head_norm, l2_k_head_norm, l2_sinks, l2_w_out)
    x = stick_breaking_layer(x, l3_norm, l3_w_in, l3_w_out)
    return x
```

```python
import functools
import math

import jax
import jax.numpy as jnp
from jax import lax
from jax.experimental import pallas as pl
from jax.experimental.pallas import tpu as pltpu

NORM_EPS = 1e-6
LOG2E = 1.4426950408889634

LANES = 128
VMEM_LIMIT_BYTES = 56 * 1024 * 1024

SB_HEADS = 16
SB_HEAD_DIM = 64
SB_WIDTH = SB_HEADS * SB_HEAD_DIM

MLA_HEADS = 8
MLA_NOPE_DIM = 128
MLA_ROPE_DIM = 64
MLA_QK_DIM = MLA_NOPE_DIM + MLA_ROPE_DIM
MLA_QK_PAD = 256
MLA_V_DIM = 128
MLA_Q_RANK = 256
MLA_KV_RANK = 128
MLA_WIDTH = MLA_HEADS * MLA_V_DIM
ROPE_THETA = 10000.0

SWA_HEADS = 16
SWA_KV_HEADS = 4
SWA_GROUP = SWA_HEADS // SWA_KV_HEADS
SWA_HEAD_DIM = 64
SWA_WINDOW = 128
SWA_WIDTH = SWA_HEADS * SWA_HEAD_DIM

F32_EXP2_ZERO_BELOW = -160.0
MASKED_LOGIT = 1e30


def _cparams(*sem):
    return pltpu.CompilerParams(dimension_semantics=sem, vmem_limit_bytes=VMEM_LIMIT_BYTES)


def _rms_scale(xf, n):
    return lax.rsqrt(jnp.sum(xf * xf, axis=-1, keepdims=True) * (1.0 / n) + NORM_EPS)


def _halves_rms(xf, gain):
    lane = lax.broadcasted_iota(jnp.int32, xf.shape, 1)
    lo = lane < LANES // 2
    sq = xf * xf
    s_lo = jnp.sum(jnp.where(lo, sq, 0.0), axis=-1, keepdims=True)
    s_hi = jnp.sum(jnp.where(lo, 0.0, sq), axis=-1, keepdims=True)
    r = lax.rsqrt(jnp.where(lo, s_lo, s_hi) * (2.0 / LANES) + NORM_EPS)
    return xf * r * gain


def _norm_matmul_kernel(x_ref, g_ref, w_ref, *refs, n_chunk, vt_cols, head_cols):
    refs = list(refs)
    hg_ref = refs.pop(0) if head_cols else None
    o_ref = refs.pop(0)
    vt_ref = refs.pop(0) if vt_cols is not None else None
    xf = x_ref[...]
    xn = (xf * _rms_scale(xf, xf.shape[-1]) * g_ref[...]).astype(jnp.bfloat16)
    n_out = o_ref.shape[-1]
    for c in range(0, n_out, n_chunk):
        acc = jnp.dot(xn, w_ref[:, c:c + n_chunk], preferred_element_type=jnp.float32)
        if any(lo < c + n_chunk and c < hi for lo, hi in head_cols):
            acc = jnp.concatenate(
                [_halves_rms(acc[:, j:j + LANES], hg_ref[:, c + j:c + j + LANES])
                 if any(lo <= c + j < hi for lo, hi in head_cols) else acc[:, j:j + LANES]
                 for j in range(0, n_chunk, LANES)], axis=1)
        o_ref[:, c:c + n_chunk] = acc.astype(o_ref.dtype)
        if vt_cols is not None:
            lo, hi = max(c, vt_cols[0]), min(c + n_chunk, vt_cols[0] + vt_cols[1])
            if lo < hi:
                vt_ref[0, lo - vt_cols[0]:hi - vt_cols[0], :] = (
                    acc[:, lo - c:hi - c].T.astype(vt_ref.dtype))


def norm_matmul(x2d, g, w_bf16, *, seq=None, vt_cols=None, head_gain=None, head_cols=(),
                tm=512, n_chunk=512):
    n, d = x2d.shape
    n_out = w_bf16.shape[1]
    in_specs = [pl.BlockSpec((tm, d), lambda i: (i, 0)),
                pl.BlockSpec((1, d), lambda i: (0, 0)),
                pl.BlockSpec((d, n_out), lambda i: (0, 0))]
    args = [x2d, g.reshape(1, d), w_bf16]
    if head_cols:
        in_specs.append(pl.BlockSpec((1, n_out), lambda i: (0, 0)))
        args.append(head_gain)
    out_specs = [pl.BlockSpec((tm, n_out), lambda i: (i, 0))]
    out_shape = [jax.ShapeDtypeStruct((n, n_out), jnp.bfloat16)]
    if vt_cols is not None:
        tiles_per_seq = seq // tm
        out_specs.append(pl.BlockSpec((1, vt_cols[1], tm),
                                      lambda i: (i // tiles_per_seq, 0, i % tiles_per_seq)))
        out_shape.append(jax.ShapeDtypeStruct((n // seq, vt_cols[1], seq), jnp.bfloat16))
    return pl.pallas_call(
        functools.partial(_norm_matmul_kernel, n_chunk=n_chunk, vt_cols=vt_cols,
                          head_cols=tuple(head_cols)),
        grid=(n // tm,),
        in_specs=in_specs,
        out_specs=out_specs,
        out_shape=out_shape,
        compiler_params=_cparams("arbitrary"),
        name="norm_matmul",
    )(*args)


def _gate_out_proj_kernel(o_ref, gate_ref, x_ref, w_ref, y_ref):
    gate = gate_ref[...].astype(jnp.float32)
    og = (o_ref[...].astype(jnp.float32) * (gate * jax.nn.sigmoid(gate))).astype(jnp.bfloat16)
    y_ref[...] = x_ref[...] + jnp.dot(og, w_ref[...], preferred_element_type=jnp.float32)


def gate_out_proj(o2d, proj2d, gate_block, x2d, w_bf16, *, tm=512):
    n, width = o2d.shape
    d = x2d.shape[1]
    return pl.pallas_call(
        _gate_out_proj_kernel,
        grid=(n // tm,),
        in_specs=[pl.BlockSpec((tm, width), lambda i: (i, 0)),
                  pl.BlockSpec((tm, width), lambda i: (i, gate_block)),
                  pl.BlockSpec((tm, d), lambda i: (i, 0)),
                  pl.BlockSpec((width, d), lambda i: (0, 0))],
        out_specs=pl.BlockSpec((tm, d), lambda i: (i, 0)),
        out_shape=jax.ShapeDtypeStruct((n, d), jnp.float32),
        compiler_params=_cparams("arbitrary"),
        name="gate_out_proj",
    )(o2d, proj2d, x2d, w_bf16)


def _sb_scores(k_blk, qm):
    return [lax.dot_general(k_blk, q_h, (((1,), (1,)), ((), ())), preferred_element_type=jnp.float32)
            for q_h in qm]


def _sb_front(z, mask, carry, tri):
    z2 = z
    if mask is not None:
        z2 = jnp.where(mask, z2, -MASKED_LOGIT)
    soft = jnp.log2(1.0 + jnp.exp2(-jnp.abs(z2)))
    ls2 = jnp.minimum(z2, 0.0) - soft
    lf2 = ls2 - z2
    later = jnp.dot(tri, lf2.astype(jnp.bfloat16), preferred_element_type=jnp.float32)
    return (ls2 if carry is None else ls2 + carry), later, jnp.sum(lf2, axis=0, keepdims=True)


def _sb_back(arg, later, vt_blk):
    return jnp.dot(vt_blk, jnp.exp2(arg + later).astype(jnp.bfloat16),
                   preferred_element_type=jnp.float32)


def _sb_attn_kernel(q_ref, k_ref, vt_ref, o_ref, arg_ref, later_ref, *, tile, head_dim):
    seq = q_ref.shape[1]
    n_tiles = seq // tile
    heads = range(LANES // head_dim)
    lane = lax.broadcasted_iota(jnp.int32, (1, LANES), 1)
    in_head = [(lane >= h * head_dim) & (lane < (h + 1) * head_dim) for h in heads]
    earlier = (lax.broadcasted_iota(jnp.int32, (tile, tile), 0)
               < lax.broadcasted_iota(jnp.int32, (tile, tile), 1))
    tri = earlier.astype(jnp.bfloat16)

    def rows(s0):
        return pl.ds(s0 if isinstance(s0, int) else pl.multiple_of(s0, tile), tile)

    def masked_q(t0):
        q = q_ref[0, rows(t0), :]
        return [jnp.where(m, q, jnp.zeros_like(q)) for m in in_head]

    def vt_tile(s0, h):
        return vt_ref[0, h * head_dim:(h + 1) * head_dim, rows(s0)]

    def blocks(z_kinds, masks, key_starts, acc):
        acc = None if acc is None else list(acc)
        carry = [None for _ in heads]
        for kind in range(2 if key_starts is not None else 0):
            for h in heads:
                b = 2 * kind + h
                acc[h] = acc[h] + _sb_back(arg_ref[b], later_ref[b], vt_tile(key_starts[kind], h))
        for kind in range(len(z_kinds) if z_kinds is not None else 0):
            for h in heads:
                b = 2 * kind + h
                arg, later, colsum = _sb_front(z_kinds[kind][h], masks[kind], carry[h], tri)
                arg_ref[b] = arg
                later_ref[b] = later
                carry[h] = colsum if carry[h] is None else carry[h] + colsum
        return acc, carry

    def write_out(t0, acc):
        o_ref[0, rows(t0), :] = jnp.concatenate(acc, axis=0).T.astype(o_ref.dtype)

    zero_acc = tuple(jnp.zeros((head_dim, tile), jnp.float32) for _ in heads)

    blocks([_sb_scores(k_ref[0, rows(0), :], masked_q(0))], [earlier], None, None)
    for h in heads:
        arg_ref[2 + h] = jnp.full((tile, tile), -MASKED_LOGIT, jnp.float32)
        later_ref[2 + h] = jnp.zeros((tile, tile), jnp.float32)

    def q_tile(i, acc_swept):
        t0 = i * tile
        qm = masked_q(t0)
        z_kinds = [_sb_scores(k_ref[0, rows(t0), :], qm), _sb_scores(k_ref[0, rows(t0 - tile), :], qm)]
        acc_prev, carry = blocks(z_kinds, [earlier, None],
                                 [t0 - tile, jnp.maximum(t0 - 2 * tile, 0)], acc_swept)
        write_out(t0 - tile, acc_prev)

        def alive(carry):
            return jnp.max(functools.reduce(jnp.maximum, carry)) >= F32_EXP2_ZERO_BELOW

        def cond(state):
            jj, live, _, _ = state
            return (jj <= i) & live

        def sweep(state):
            jj, _, acc, carry = state
            s0 = t0 - jj * tile
            z = _sb_scores(k_ref[0, rows(s0), :], qm)
            acc, carry = list(acc), list(carry)
            for h in heads:
                arg, later, colsum = _sb_front(z[h], None, carry[h], tri)
                acc[h] = acc[h] + _sb_back(arg, later, vt_tile(s0, h))
                carry[h] = carry[h] + colsum
            return jj + 1, alive(carry), tuple(acc), tuple(carry)

        _, _, acc, _ = lax.while_loop(cond, sweep, (jnp.int32(2), alive(carry), zero_acc,
                                                    tuple(carry)))
        return acc

    acc_swept = lax.fori_loop(1, n_tiles, q_tile, zero_acc)
    last = (n_tiles - 1) * tile
    write_out(last, blocks(None, None, [last, max(last - tile, 0)], acc_swept)[0])


def sb_attention(proj3d, vt3d, *, tile=256):
    b, s, _ = proj3d.shape
    cols = SB_WIDTH // LANES
    blk = lambda off: pl.BlockSpec((1, s, LANES), lambda bi, p: (bi, 0, off + p))
    return pl.pallas_call(
        functools.partial(_sb_attn_kernel, tile=tile, head_dim=SB_HEAD_DIM),
        grid=(b, cols),
        in_specs=[blk(0), blk(cols), pl.BlockSpec((1, LANES, s), lambda bi, p: (bi, p, 0))],
        out_specs=pl.BlockSpec((1, s, LANES), lambda bi, p: (bi, 0, p)),
        out_shape=jax.ShapeDtypeStruct((b, s, SB_WIDTH), jnp.bfloat16),
        scratch_shapes=[pltpu.VMEM((2 * LANES // SB_HEAD_DIM, tile, tile), jnp.float32),
                        pltpu.VMEM((2 * LANES // SB_HEAD_DIM, tile, tile), jnp.float32)],
        compiler_params=_cparams("arbitrary", "arbitrary"),
        name="sb_attention",
    )(proj3d, proj3d, vt3d)


def stick_breaking_layer(x, norm_g, w_in, w_out):
    b, s, d = x.shape
    x2d = x.reshape(b * s, d)
    w_in_l = w_in.at[:, :SB_WIDTH].multiply(LOG2E / math.sqrt(SB_HEAD_DIM))
    proj, vt = norm_matmul(x2d, norm_g, w_in_l.astype(jnp.bfloat16), seq=s,
                           vt_cols=(2 * SB_WIDTH, SB_WIDTH))
    o = sb_attention(proj.reshape(b, s, -1), vt)
    y = gate_out_proj(o.reshape(b * s, SB_WIDTH), proj, 3, x2d, w_out.astype(jnp.bfloat16))
    return y.reshape(b, s, d)


def _rope_tables(seq):
    half = MLA_ROPE_DIM // 2
    inv_freq = ROPE_THETA ** (-jnp.arange(half, dtype=jnp.float32) / half)
    ang = jnp.arange(seq, dtype=jnp.float32)[:, None] * inv_freq[None, :]
    cos, sin = jnp.cos(ang), jnp.sin(ang)
    zero = jnp.zeros((seq, LANES - MLA_ROPE_DIM), jnp.float32)
    return (jnp.concatenate([cos, cos, zero], axis=1),
            jnp.concatenate([-sin, sin, zero], axis=1))


def _rotate_half(pe):
    lane = lax.broadcasted_iota(jnp.int32, pe.shape, 1)
    half = MLA_ROPE_DIM // 2
    return jnp.where(lane < half, pltpu.roll(pe, LANES - half, axis=1), pltpu.roll(pe, half, axis=1))


def _mla_prep_kernel(qlat_ref, kvl_ref, cos_ref, sin_ref, qa_ref, kva_ref, gq_ref, gk_ref,
                     wuq_ref, wukv_ref, q_ref, k_ref, vt_ref):
    cos, sin = cos_ref[...], sin_ref[...]
    ql = qlat_ref[...].astype(jnp.float32)
    qn = (ql * _rms_scale(ql, MLA_Q_RANK) * qa_ref[...]).astype(jnp.bfloat16)
    kvl = kvl_ref[...].astype(jnp.float32)
    kv_lat = kvl[:, :MLA_KV_RANK]
    kvn = (kv_lat * _rms_scale(kv_lat, MLA_KV_RANK) * kva_ref[...]).astype(jnp.bfloat16)
    gq, gk = gq_ref[...], gk_ref[...]
    k_pe = kvl[:, MLA_KV_RANK:]
    k_pe_sq = jnp.sum(k_pe * k_pe, axis=-1, keepdims=True)
    k_pe_g = k_pe * gk[:, MLA_NOPE_DIM:]
    k_pe_rot = k_pe_g * cos + _rotate_half(k_pe_g) * sin
    for h in range(MLA_HEADS):
        c0 = h * MLA_QK_PAD
        qh = jnp.dot(qn, wuq_ref[:, c0:c0 + MLA_QK_PAD], preferred_element_type=jnp.float32)
        q_pe = qh[:, MLA_NOPE_DIM:]
        q_nope = qh[:, :MLA_NOPE_DIM]
        q_ss = jnp.sum(q_nope * q_nope + 0.5 * (q_pe * q_pe), axis=-1, keepdims=True)
        r = lax.rsqrt(q_ss * (1.0 / MLA_QK_DIM) + NORM_EPS)
        q_ref[:, c0:c0 + MLA_NOPE_DIM] = (q_nope * r * gq[:, :MLA_NOPE_DIM]).astype(q_ref.dtype)
        q_pe = q_pe * r * gq[:, MLA_NOPE_DIM:]
        q_ref[:, c0 + MLA_NOPE_DIM:c0 + MLA_QK_PAD] = (
            q_pe * cos + pltpu.roll(q_pe, MLA_ROPE_DIM // 2, axis=1) * sin).astype(q_ref.dtype)
        kvh = jnp.dot(kvn, wukv_ref[:, c0:c0 + MLA_QK_PAD], preferred_element_type=jnp.float32)
        k_nope = kvh[:, :MLA_NOPE_DIM]
        r = lax.rsqrt((jnp.sum(k_nope * k_nope, axis=-1, keepdims=True) + k_pe_sq)
                      * (1.0 / MLA_QK_DIM) + NORM_EPS)
        k_ref[:, c0:c0 + MLA_NOPE_DIM] = (k_nope * r * gk[:, :MLA_NOPE_DIM]).astype(k_ref.dtype)
        k_ref[:, c0 + MLA_NOPE_DIM:c0 + MLA_QK_PAD] = (k_pe_rot * r).astype(k_ref.dtype)
        vt_ref[0, h * MLA_V_DIM:(h + 1) * MLA_V_DIM, :] = kvh[:, MLA_NOPE_DIM:].T.astype(vt_ref.dtype)


def mla_prep(proj2d, seq, q_a_norm, kv_a_norm, q_head_norm, k_head_norm, wuq_pad, wukv, *, tm=512):
    n = proj2d.shape[0]
    cos, sin = _rope_tables(seq)
    pad = jnp.zeros((MLA_QK_PAD - MLA_QK_DIM,), jnp.float32)
    gq = q_head_norm * (LOG2E / math.sqrt(MLA_QK_DIM))
    gq = jnp.concatenate([gq, gq[MLA_NOPE_DIM:]]).reshape(1, MLA_QK_PAD)
    gk = jnp.concatenate([k_head_norm, pad]).reshape(1, MLA_QK_PAD)
    qlat_blk = MLA_WIDTH // MLA_Q_RANK
    tiles_per_seq = seq // tm
    row = lambda i: (i, 0)
    const = lambda i: (0, 0)
    pos = lambda i: (i % tiles_per_seq, 0)
    return pl.pallas_call(
        _mla_prep_kernel,
        grid=(n // tm,),
        in_specs=[pl.BlockSpec((tm, MLA_Q_RANK), lambda i: (i, qlat_blk)),
                  pl.BlockSpec((tm, MLA_Q_RANK), lambda i: (i, qlat_blk + 1)),
                  pl.BlockSpec((tm, LANES), pos),
                  pl.BlockSpec((tm, LANES), pos),
                  pl.BlockSpec((1, MLA_Q_RANK), const),
                  pl.BlockSpec((1, MLA_KV_RANK), const),
                  pl.BlockSpec((1, MLA_QK_PAD), const),
                  pl.BlockSpec((1, MLA_QK_PAD), const),
                  pl.BlockSpec((MLA_Q_RANK, MLA_HEADS * MLA_QK_PAD), const),
                  pl.BlockSpec((MLA_KV_RANK, MLA_HEADS * MLA_QK_PAD), const)],
        out_specs=[pl.BlockSpec((tm, MLA_HEADS * MLA_QK_PAD), row),
                   pl.BlockSpec((tm, MLA_HEADS * MLA_QK_PAD), row),
                   pl.BlockSpec((1, MLA_WIDTH, tm),
                                lambda i: (i // tiles_per_seq, 0, i % tiles_per_seq))],
        out_shape=[jax.ShapeDtypeStruct((n, MLA_HEADS * MLA_QK_PAD), jnp.bfloat16),
                   jax.ShapeDtypeStruct((n, MLA_HEADS * MLA_QK_PAD), jnp.bfloat16),
                   jax.ShapeDtypeStruct((n // seq, MLA_WIDTH, seq), jnp.bfloat16)],
        compiler_params=_cparams("arbitrary"),
        name="mla_prep",
    )(proj2d, proj2d, cos, sin, q_a_norm.reshape(1, -1), kv_a_norm.reshape(1, -1), gq, gk,
      wuq_pad, wukv)


def _causal_attn_kernel(q_ref, k_ref, vt_ref, o_ref, s_ref, p_ref, *, tile):
    seq = q_ref.shape[1]
    chains = range(2)
    group = 2 * tile
    causal = (lax.broadcasted_iota(jnp.int32, (tile, tile), 0)
              <= lax.broadcasted_iota(jnp.int32, (tile, tile), 1))

    def qk(q, s0):
        return lax.dot_general(k_ref[0, pl.ds(pl.multiple_of(s0, tile), tile), :], q,
                               (((1,), (1,)), ((), ())), preferred_element_type=jnp.float32)

    def pv(p, s0):
        return jnp.dot(vt_ref[0, :, pl.ds(pl.multiple_of(s0, tile), tile)], p,
                       preferred_element_type=jnp.float32)

    def softmax_step(s2, m, l, mask):
        if mask is not None:
            s2 = jnp.where(mask, s2, -jnp.inf)
        m_new = jnp.maximum(m, jnp.max(s2, axis=0, keepdims=True))
        p = jnp.exp2(s2 - m_new)
        alpha = jnp.exp2(m - m_new)
        return m_new, alpha * l + jnp.sum(p, axis=0, keepdims=True), alpha, p.astype(jnp.bfloat16)

    def start_group(t0):
        for c in chains:
            r0 = t0 + c * tile
            r0 = r0 if isinstance(r0, int) else pl.multiple_of(r0, tile)
            s_ref[c] = qk(q_ref[0, pl.ds(r0, tile), :], 0)

    for c in chains:
        p_ref[c] = jnp.zeros((tile, tile), p_ref.dtype)
    start_group(0)

    def q_group(i, _):
        t0 = pl.multiple_of(i * group, group)
        q = [q_ref[0, pl.ds(t0 + c * tile, tile), :] for c in chains]
        init = tuple((jnp.full((1, tile), -jnp.inf, jnp.float32), jnp.zeros((1, tile), jnp.float32),
                      jnp.zeros((vt_ref.shape[1], tile), jnp.float32),
                      jnp.ones((1, tile), jnp.float32)) for _ in chains)

        def first_half(j, state, s_b_chains, masks):
            b_prev = jnp.maximum(j - 1, 0) * group + tile
            pv_b = [pv(p_ref[c], b_prev) for c in chains]
            s_b = {c: qk(q[c], j * group + tile) for c in s_b_chains}
            out = []
            for c in chains:
                m, l, acc, alpha_b = state[c]
                acc = alpha_b * acc + jnp.where(j > 0, pv_b[c], 0.0)
                m, l, alpha_a, p_a = softmax_step(s_ref[c], m, l, masks[c])
                out.append((m, l, acc, alpha_a, p_a))
            return out, s_b

        def full_tiles(j, state):
            mid, s_b = first_half(j, state, chains, (None, None))
            pv_a = [pv(mid[c][4], j * group) for c in chains]
            for c in chains:
                s_ref[c] = qk(q[c], (j + 1) * group)
            out = []
            for c in chains:
                m, l, acc, alpha_a, _ = mid[c]
                acc = alpha_a * acc + pv_a[c]
                m, l, alpha_b, p_b = softmax_step(s_b[c], m, l, None)
                p_ref[c] = p_b
                out.append((m, l, acc, alpha_b))
            return tuple(out)

        state = lax.fori_loop(0, i, full_tiles, init)
        mid, s_b = first_half(i, state, (1,), (causal, None))
        start_group(jnp.minimum(t0 + group, seq - group))
        pv_a = [pv(mid[c][4], t0) for c in chains]
        m, l, acc, alpha_a, _ = mid[0]
        o_ref[0, pl.ds(t0, tile), :] = ((alpha_a * acc + pv_a[0]) / l).T.astype(o_ref.dtype)
        m, l, acc, alpha_a, _ = mid[1]
        acc = alpha_a * acc + pv_a[1]
        m, l, alpha_b, p_b = softmax_step(s_b[1], m, l, causal)
        acc = alpha_b * acc + pv(p_b, t0 + tile)
        o_ref[0, pl.ds(t0 + tile, tile), :] = (acc / l).T.astype(o_ref.dtype)
        return 0

    lax.fori_loop(0, seq // group, q_group, 0)


def mla_attention(q3d, k3d, vt3d, *, tile=1024):
    b, s, _ = q3d.shape
    qk = lambda bi, h: (bi, 0, h)
    return pl.pallas_call(
        functools.partial(_causal_attn_kernel, tile=tile),
        grid=(b, MLA_HEADS),
        in_specs=[pl.BlockSpec((1, s, MLA_QK_PAD), qk),
                  pl.BlockSpec((1, s, MLA_QK_PAD), qk),
                  pl.BlockSpec((1, MLA_V_DIM, s), lambda bi, h: (bi, h, 0))],
        out_specs=pl.BlockSpec((1, s, MLA_V_DIM), qk),
        out_shape=jax.ShapeDtypeStruct((b, s, MLA_WIDTH), jnp.bfloat16),
        scratch_shapes=[pltpu.VMEM((2, tile, tile), jnp.float32),
                        pltpu.VMEM((2, tile, tile), jnp.bfloat16)],
        compiler_params=_cparams("arbitrary", "arbitrary"),
        name="mla_attention",
    )(q3d, k3d, vt3d)


def mla_layer(x, norm_g, w_in, q_a_norm, w_uq, kv_a_norm, w_ukv, q_head_norm, k_head_norm, w_out):
    b, s, d = x.shape
    x2d = x.reshape(b * s, d)
    c1 = MLA_Q_RANK
    c3 = c1 + MLA_KV_RANK + MLA_ROPE_DIM
    w_in_l = jnp.concatenate(
        [w_in[:, c3:], w_in[:, :c3], jnp.zeros((d, LANES - MLA_ROPE_DIM), w_in.dtype)], axis=1)
    wuq3 = w_uq.reshape(MLA_Q_RANK, MLA_HEADS, MLA_QK_DIM)
    wuq_pad = jnp.concatenate([wuq3, wuq3[:, :, MLA_NOPE_DIM:]], axis=2).reshape(MLA_Q_RANK, -1)
    proj, = norm_matmul(x2d, norm_g, w_in_l.astype(jnp.bfloat16))
    q, k, vt = mla_prep(proj, s, q_a_norm, kv_a_norm, q_head_norm, k_head_norm,
                        wuq_pad.astype(jnp.bfloat16), w_ukv.astype(jnp.bfloat16))
    o = mla_attention(q.reshape(b, s, -1), k.reshape(b, s, -1), vt)
    y = gate_out_proj(o.reshape(b * s, MLA_WIDTH), proj, 0, x2d, w_out.astype(jnp.bfloat16))
    return y.reshape(b, s, d)


def _swa_attn_kernel(q_ref, k_ref, v_ref, slope_ref, sink_ref, o_ref, kn_ref, vt_ref, bias_ref,
                     s_ref, *, tq):
    seq = q_ref.shape[1]
    ncol = q_ref.shape[2] // LANES
    halves = range(LANES // SWA_HEAD_DIM)
    col_pairs = range(ncol // 2)
    tk = tq + SWA_WINDOW
    pair = pl.program_id(1)
    kn_ref[pl.ds(0, SWA_WINDOW), :] = jnp.zeros((SWA_WINDOW, LANES), kn_ref.dtype)
    kn_ref[pl.ds(SWA_WINDOW, seq), :] = k_ref[0]
    vt_ref[:, pl.ds(0, SWA_WINDOW)] = jnp.zeros((LANES, SWA_WINDOW), vt_ref.dtype)
    vt_ref[:, pl.ds(SWA_WINDOW, seq)] = v_ref[0]
    lane = lax.broadcasted_iota(jnp.int32, (1, LANES), 1)
    in_half = [(lane >= h * SWA_HEAD_DIM) & (lane < (h + 1) * SWA_HEAD_DIM) for h in halves]
    key_row = lax.broadcasted_iota(jnp.int32, (tk, tq), 0)
    rel = lax.broadcasted_iota(jnp.int32, (tk, tq), 1) + SWA_WINDOW - key_row
    in_window = (rel >= 0) & (rel < SWA_WINDOW)
    rel_f = rel.astype(jnp.float32)
    for h in halves:
        for c in range(ncol):
            bias_ref[h, :, c * tq:(c + 1) * tq] = jnp.where(
                in_window, -slope_ref[pair, h, c] * rel_f, -jnp.inf)
    sink_row = [[jnp.concatenate([jnp.full((1, tq), sink_ref[pair, h, 2 * cp + cc], jnp.float32)
                                  for cc in range(2)], axis=1) for cp in col_pairs] for h in halves]
    before_start = jnp.concatenate([key_row < SWA_WINDOW] * 2, axis=1)

    chains = [(h, cp) for h in halves for cp in col_pairs]

    def scores(t0):
        kb = kn_ref[pl.ds(t0, tk), :]
        qn = [q_ref[0, pl.ds(t0, tq), c * LANES:(c + 1) * LANES] for c in range(ncol)]
        return [lax.dot_general(
            kb, jnp.concatenate([jnp.where(in_half[h], qn[2 * cp + cc], jnp.zeros_like(qn[0]))
                                 for cc in range(2)], axis=0),
            (((1,), (1,)), ((), ())), preferred_element_type=jnp.float32) for h, cp in chains]

    for idx, s0 in enumerate(scores(0)):
        s_ref[idx] = s0

    def q_tile(i, first):
        t0 = pl.multiple_of(i * tq, tq)
        s_next = scores(pl.multiple_of(jnp.minimum(t0 + tq, seq - tq), tq))
        o_t = {}
        for idx, (h, cp) in enumerate(chains):
            s = s_ref[idx] + bias_ref[h, :, 2 * cp * tq:2 * (cp + 1) * tq]
            s_ref[idx] = s_next[idx]
            if first:
                s = jnp.where(before_start, -jnp.inf, s)
            m = jnp.maximum(jnp.max(s, axis=0, keepdims=True), sink_row[h][cp])
            e = jnp.exp(s - m)
            denom = jnp.sum(e, axis=0, keepdims=True) + jnp.exp(sink_row[h][cp] - m)
            pv = jnp.dot(vt_ref[h * SWA_HEAD_DIM:(h + 1) * SWA_HEAD_DIM, pl.ds(t0, tk)],
                         e.astype(jnp.bfloat16), preferred_element_type=jnp.float32)
            o_t[h, cp] = pv / denom
        for c in range(ncol):
            cp, cc = divmod(c, 2)
            o_col_t = jnp.concatenate([o_t[h, cp][:, cc * tq:(cc + 1) * tq] for h in halves], axis=0)
            o_ref[0, pl.ds(t0, tq), c * LANES:(c + 1) * LANES] = o_col_t.T.astype(o_ref.dtype)

    q_tile(0, True)

    def q_tile_loop(i, _):
        q_tile(i, False)
        return 0

    lax.fori_loop(1, seq // tq, q_tile_loop, 0)


def swa_attention(proj3d, vt3d, slopes, sinks, *, tq=128):
    b, s, _ = proj3d.shape
    npair = SWA_KV_HEADS // 2
    qw = SWA_WIDTH // npair
    kcol0 = 2 * SWA_WIDTH // LANES
    smem = pl.BlockSpec(memory_space=pltpu.SMEM)
    return pl.pallas_call(
        functools.partial(_swa_attn_kernel, tq=tq),
        grid=(b, npair),
        in_specs=[pl.BlockSpec((1, s, qw), lambda bi, p: (bi, 0, p)),
                  pl.BlockSpec((1, s, LANES), lambda bi, p: (bi, 0, kcol0 + p)),
                  pl.BlockSpec((1, LANES, s), lambda bi, p: (bi, p, 0)),
                  smem, smem],
        out_specs=pl.BlockSpec((1, s, qw), lambda bi, p: (bi, 0, p)),
        out_shape=jax.ShapeDtypeStruct((b, s, SWA_WIDTH), jnp.bfloat16),
        scratch_shapes=[pltpu.VMEM((s + SWA_WINDOW, LANES), jnp.bfloat16),
                        pltpu.VMEM((LANES, s + SWA_WINDOW), jnp.bfloat16),
                        pltpu.VMEM((LANES // SWA_HEAD_DIM, tq + SWA_WINDOW, qw // LANES * tq),
                                   jnp.float32),
                        pltpu.VMEM((qw // LANES, tq + SWA_WINDOW, 2 * tq), jnp.float32)],
        compiler_params=_cparams("arbitrary", "arbitrary"),
        name="swa_attention",
    )(proj3d, proj3d, vt3d, slopes, sinks)


def _swa_head_order():
    order = []
    for p in range(SWA_KV_HEADS // 2):
        for c in range(SWA_GROUP):
            order += [2 * p * SWA_GROUP + c, (2 * p + 1) * SWA_GROUP + c]
    return order


def swa_layer(x, norm_g, w_in, q_head_norm, k_head_norm, sinks, w_out):
    b, s, d = x.shape
    x2d = x.reshape(b * s, d)
    kv_w = SWA_KV_HEADS * SWA_HEAD_DIM
    c1, c2, c3 = SWA_WIDTH, SWA_WIDTH + kv_w, SWA_WIDTH + 2 * kv_w
    order = jnp.asarray(_swa_head_order())
    perm = (order[:, None] * SWA_HEAD_DIM + jnp.arange(SWA_HEAD_DIM)[None, :]).reshape(-1)
    w_in_l = jnp.concatenate([w_in[:, :c1][:, perm], w_in[:, c3:][:, perm], w_in[:, c1:c3]], axis=1)
    w_out_l = w_out[perm, :]
    slopes = 2.0 ** (-8.0 * jnp.arange(1, SWA_HEADS + 1, dtype=jnp.float32) / SWA_HEADS)
    by_pos = lambda t: t[order].reshape(SWA_KV_HEADS // 2, SWA_GROUP, 2).transpose(0, 2, 1)
    q_cols, k_cols = (0, SWA_WIDTH), (2 * SWA_WIDTH, 2 * SWA_WIDTH + kv_w)
    head_gain = jnp.zeros((1, w_in_l.shape[1]), jnp.float32)
    head_gain = head_gain.at[0, q_cols[0]:q_cols[1]].set(
        jnp.tile(q_head_norm, SWA_HEADS) / math.sqrt(SWA_HEAD_DIM))
    head_gain = head_gain.at[0, k_cols[0]:k_cols[1]].set(jnp.tile(k_head_norm, SWA_KV_HEADS))
    proj, vt = norm_matmul(x2d, norm_g, w_in_l.astype(jnp.bfloat16), seq=s,
                           vt_cols=(2 * SWA_WIDTH + kv_w, kv_w),
                           head_gain=head_gain, head_cols=(q_cols, k_cols))
    o = swa_attention(proj.reshape(b, s, -1), vt, by_pos(slopes), by_pos(sinks.astype(jnp.float32)))
    y = gate_out_proj(o.reshape(b * s, SWA_WIDTH), proj, 1, x2d, w_out_l.astype(jnp.bfloat16))
    return y.reshape(b, s, d)


def kernel(x, l0_norm, l0_w_in, l0_w_out, l1_norm, l1_w_in, l1_q_a_norm, l1_w_uq, l1_kv_a_norm,
           l1_w_ukv, l1_q_head_norm, l1_k_head_norm, l1_w_out, l2_norm, l2_w_in, l2_q_head_norm,
           l2_k_head_norm, l2_sinks, l2_w_out, l3_norm, l3_w_in, l3_w_out):
    x = stick_breaking_layer(x, l0_norm, l0_w_in, l0_w_out)
    x = mla_layer(x, l1_norm, l1_w_in, l1_q_a_norm, l1_w_uq, l1_kv_a_norm, l1_w_ukv,
                  l1_q_head_norm, l1_k_head_norm, l1_w_out)
    x = swa_layer(x, l2_norm, l2_w_in, l2_q_head_norm, l2_k_head_norm, l2_sinks, l2_w_out)
    x = stick_breaking_layer(x, l3_norm, l3_w_in, l3_w_out)
    return x
```

```python
import functools
import math

import jax
import jax.numpy as jnp
from jax import lax
from jax.experimental import pallas as pl
from jax.experimental.pallas import tpu as pltpu

NORM_EPS = 1e-6
LOG2E = 1.4426950408889634

LANES = 128
VMEM_LIMIT_BYTES = 56 * 1024 * 1024

SB_HEADS = 16
SB_HEAD_DIM = 64
SB_WIDTH = SB_HEADS * SB_HEAD_DIM

MLA_HEADS = 8
MLA_NOPE_DIM = 128
MLA_ROPE_DIM = 64
MLA_QK_DIM = MLA_NOPE_DIM + MLA_ROPE_DIM
MLA_QK_PAD = 256
MLA_V_DIM = 128
MLA_Q_RANK = 256
MLA_KV_RANK = 128
MLA_WIDTH = MLA_HEADS * MLA_V_DIM
ROPE_THETA = 10000.0

SWA_HEADS = 16
SWA_KV_HEADS = 4
SWA_GROUP = SWA_HEADS // SWA_KV_HEADS
SWA_HEAD_DIM = 64
SWA_WINDOW = 128
SWA_WIDTH = SWA_HEADS * SWA_HEAD_DIM

F32_EXP2_ZERO_BELOW = -160.0
MASKED_LOGIT = 1e30


def _cparams(*sem):
    return pltpu.CompilerParams(dimension_semantics=sem, vmem_limit_bytes=VMEM_LIMIT_BYTES)


def _rms_scale(xf, n):
    return lax.rsqrt(jnp.sum(xf * xf, axis=-1, keepdims=True) * (1.0 / n) + NORM_EPS)


def _halves_rms(xf, gain):
    lane = lax.broadcasted_iota(jnp.int32, xf.shape, 1)
    lo = lane < LANES // 2
    sq = xf * xf
    s_lo = jnp.sum(jnp.where(lo, sq, 0.0), axis=-1, keepdims=True)
    s_hi = jnp.sum(jnp.where(lo, 0.0, sq), axis=-1, keepdims=True)
    r = lax.rsqrt(jnp.where(lo, s_lo, s_hi) * (2.0 / LANES) + NORM_EPS)
    return xf * r * gain


def _norm_matmul_kernel(x_ref, g_ref, w_ref, *refs, n_chunk, vt_cols, head_cols):
    refs = list(refs)
    hg_ref = refs.pop(0) if head_cols else None
    o_ref = refs.pop(0)
    vt_ref = refs.pop(0) if vt_cols is not None else None
    xf = x_ref[...]
    xn = (xf * _rms_scale(xf, xf.shape[-1]) * g_ref[...]).astype(jnp.bfloat16)
    n_out = o_ref.shape[-1]
    for c in range(0, n_out, n_chunk):
        acc = jnp.dot(xn, w_ref[:, c:c + n_chunk], preferred_element_type=jnp.float32)
        if any(lo < c + n_chunk and c < hi for lo, hi in head_cols):
            acc = jnp.concatenate(
                [_halves_rms(acc[:, j:j + LANES], hg_ref[:, c + j:c + j + LANES])
                 if any(lo <= c + j < hi for lo, hi in head_cols) else acc[:, j:j + LANES]
                 for j in range(0, n_chunk, LANES)], axis=1)
        o_ref[:, c:c + n_chunk] = acc.astype(o_ref.dtype)
        if vt_cols is not None:
            lo, hi = max(c, vt_cols[0]), min(c + n_chunk, vt_cols[0] + vt_cols[1])
            if lo < hi:
                vt_ref[0, lo - vt_cols[0]:hi - vt_cols[0], :] = (
                    acc[:, lo - c:hi - c].T.astype(vt_ref.dtype))


def norm_matmul(x2d, g, w_bf16, *, seq=None, vt_cols=None, head_gain=None, head_cols=(),
                tm=512, n_chunk=512):
    n, d = x2d.shape
    n_out = w_bf16.shape[1]
    in_specs = [pl.BlockSpec((tm, d), lambda i: (i, 0)),
                pl.BlockSpec((1, d), lambda i: (0, 0)),
                pl.BlockSpec((d, n_out), lambda i: (0, 0))]
    args = [x2d, g.reshape(1, d), w_bf16]
    if head_cols:
        in_specs.append(pl.BlockSpec((1, n_out), lambda i: (0, 0)))
        args.append(head_gain)
    out_specs = [pl.BlockSpec((tm, n_out), lambda i: (i, 0))]
    out_shape = [jax.ShapeDtypeStruct((n, n_out), jnp.bfloat16)]
    if vt_cols is not None:
        tiles_per_seq = seq // tm
        out_specs.append(pl.BlockSpec((1, vt_cols[1], tm),
                                      lambda i: (i // tiles_per_seq, 0, i % tiles_per_seq)))
        out_shape.append(jax.ShapeDtypeStruct((n // seq, vt_cols[1], seq), jnp.bfloat16))
    return pl.pallas_call(
        functools.partial(_norm_matmul_kernel, n_chunk=n_chunk, vt_cols=vt_cols,
                          head_cols=tuple(head_cols)),
        grid=(n // tm,),
        in_specs=in_specs,
        out_specs=out_specs,
        out_shape=out_shape,
        compiler_params=_cparams("arbitrary"),
        name="norm_matmul",
    )(*args)


def _gate_out_proj_kernel(o_ref, gate_ref, x_ref, w_ref, y_ref):
    gate = gate_ref[...].astype(jnp.float32)
    og = (o_ref[...].astype(jnp.float32) * (gate * jax.nn.sigmoid(gate))).astype(jnp.bfloat16)
    y_ref[...] = x_ref[...] + jnp.dot(og, w_ref[...], preferred_element_type=jnp.float32)


def gate_out_proj(o2d, proj2d, gate_block, x2d, w_bf16, *, tm=512):
    n, width = o2d.shape
    d = x2d.shape[1]
    return pl.pallas_call(
        _gate_out_proj_kernel,
        grid=(n // tm,),
        in_specs=[pl.BlockSpec((tm, width), lambda i: (i, 0)),
                  pl.BlockSpec((tm, width), lambda i: (i, gate_block)),
                  pl.BlockSpec((tm, d), lambda i: (i, 0)),
                  pl.BlockSpec((width, d), lambda i: (0, 0))],
        out_specs=pl.BlockSpec((tm, d), lambda i: (i, 0)),
        out_shape=jax.ShapeDtypeStruct((n, d), jnp.float32),
        compiler_params=_cparams("arbitrary"),
        name="gate_out_proj",
    )(o2d, proj2d, x2d, w_bf16)


def _sb_scores(k_blk, qm):
    return [lax.dot_general(k_blk, q_h, (((1,), (1,)), ((), ())), preferred_element_type=jnp.float32)
            for q_h in qm]


def _sb_front(z, mask, carry, tri):
    z2 = z
    if mask is not None:
        z2 = jnp.where(mask, z2, -MASKED_LOGIT)
    soft = jnp.log2(1.0 + jnp.exp2(-jnp.abs(z2)))
    ls2 = jnp.minimum(z2, 0.0) - soft
    lf2 = ls2 - z2
    later = jnp.dot(tri, lf2.astype(jnp.bfloat16), preferred_element_type=jnp.float32)
    return (ls2 if carry is None else ls2 + carry), later, jnp.sum(lf2, axis=0, keepdims=True)


def _sb_back(arg, later, vt_blk):
    return jnp.dot(vt_blk, jnp.exp2(arg + later).astype(jnp.bfloat16),
                   preferred_element_type=jnp.float32)


def _sb_attn_kernel(q_ref, k_ref, vt_ref, o_ref, arg_ref, later_ref, *, tile, head_dim, pair):
    seq = q_ref.shape[1]
    n_groups = seq // (pair * tile)
    heads = range(LANES // head_dim)
    lane = lax.broadcasted_iota(jnp.int32, (1, LANES), 1)
    in_head = [(lane >= h * head_dim) & (lane < (h + 1) * head_dim) for h in heads]
    earlier = (lax.broadcasted_iota(jnp.int32, (tile, tile), 0)
               < lax.broadcasted_iota(jnp.int32, (tile, tile), 1))
    tri = earlier.astype(jnp.bfloat16)
    masks = (earlier, None)

    def rows(s0):
        return pl.ds(s0 if isinstance(s0, int) else pl.multiple_of(s0, tile), tile)

    def masked_q(t0):
        q = q_ref[0, rows(t0), :]
        return [jnp.where(m, q, jnp.zeros_like(q)) for m in in_head]

    def vt_tile(s0, h):
        return vt_ref[0, h * head_dim:(h + 1) * head_dim, rows(s0)]

    def slot(p, kind, h):
        return (2 * p + kind) * len(heads) + h

    def before(t0, kind):
        s0 = t0 - kind * tile
        return max(s0, 0) if isinstance(s0, int) else jnp.maximum(s0, 0)

    def scores(g0, kinds_of):
        qm = [masked_q(g0 + p * tile) for p in range(pair)]
        return [[_sb_scores(k_ref[0, rows(before(g0 + p * tile, kind)), :], qm[p])
                 for kind in kinds_of[p]] for p in range(pair)]

    def park(z):
        carries = []
        for p in range(pair):
            carry = [None for _ in heads]
            for kind, z_kind in enumerate(z[p]):
                for h in heads:
                    arg, later, colsum = _sb_front(z_kind[h], masks[kind], carry[h], tri)
                    arg_ref[slot(p, kind, h)] = arg
                    later_ref[slot(p, kind, h)] = later
                    carry[h] = colsum if carry[h] is None else carry[h] + colsum
            carries.append(carry)
        return carries

    def finish(g0, acc_swept):
        for p in range(pair):
            t0 = g0 + p * tile
            acc = list(acc_swept[p])
            for kind in range(2):
                for h in heads:
                    acc[h] = acc[h] + _sb_back(arg_ref[slot(p, kind, h)], later_ref[slot(p, kind, h)],
                                               vt_tile(before(t0, kind), h))
            o_ref[0, rows(t0), :] = jnp.concatenate(acc, axis=0).T.astype(o_ref.dtype)

    zero_acc = tuple(jnp.zeros((head_dim, tile), jnp.float32) for _ in heads)

    def alive(carry):
        return jnp.max(functools.reduce(jnp.maximum, carry)) >= F32_EXP2_ZERO_BELOW

    def sweeps(g0, first_index, carries):
        live_any = alive([c for carry in carries for c in carry])
        swept = []
        for p in range(pair):
            last = first_index + p
            if isinstance(last, int) and last < 2:
                swept.append(zero_acc)
                continue
            t0 = g0 + p * tile

            def cond(state, last=last):
                jj, live, _, _ = state
                return (jj <= last) & live

            def sweep(state, t0=t0):
                jj, _, acc, carry = state
                s0 = t0 - jj * tile
                z_s = _sb_scores(k_ref[0, rows(s0), :], masked_q(t0))
                acc, carry = list(acc), list(carry)
                for h in heads:
                    arg, later, colsum = _sb_front(z_s[h], None, carry[h], tri)
                    acc[h] = acc[h] + _sb_back(arg, later, vt_tile(s0, h))
                    carry[h] = carry[h] + colsum
                return jj + 1, alive(carry), tuple(acc), tuple(carry)

            _, _, acc, _ = lax.while_loop(cond, sweep, (jnp.int32(2), live_any, zero_acc,
                                                        tuple(carries[p])))
            swept.append(acc)
        return tuple(swept)

    carries = park(scores(0, [(0,)] + [(0, 1)] * (pair - 1)))
    for h in heads:
        arg_ref[slot(0, 1, h)] = jnp.full((tile, tile), -MASKED_LOGIT, jnp.float32)
        later_ref[slot(0, 1, h)] = jnp.zeros((tile, tile), jnp.float32)
    acc_swept = sweeps(0, 0, carries)

    def group(g, acc_swept):
        g0 = g * (pair * tile)
        z = scores(g0, [(0, 1)] * pair)
        finish(g0 - pair * tile, acc_swept)
        return sweeps(g0, g * pair, park(z))

    acc_swept = lax.fori_loop(1, n_groups, group, acc_swept)
    finish((n_groups - 1) * pair * tile, acc_swept)


def sb_attention(proj3d, vt3d, *, tile=256, pair=2):
    b, s, _ = proj3d.shape
    cols = SB_WIDTH // LANES
    blk = lambda off: pl.BlockSpec((1, s, LANES), lambda bi, p: (bi, 0, off + p))
    return pl.pallas_call(
        functools.partial(_sb_attn_kernel, tile=tile, head_dim=SB_HEAD_DIM, pair=pair),
        grid=(b, cols),
        in_specs=[blk(0), blk(cols), pl.BlockSpec((1, LANES, s), lambda bi, p: (bi, p, 0))],
        out_specs=pl.BlockSpec((1, s, LANES), lambda bi, p: (bi, 0, p)),
        out_shape=jax.ShapeDtypeStruct((b, s, SB_WIDTH), jnp.bfloat16),
        scratch_shapes=[pltpu.VMEM((pair * 2 * LANES // SB_HEAD_DIM, tile, tile), jnp.float32)] * 2,
        compiler_params=_cparams("arbitrary", "arbitrary"),
        name="sb_attention",
    )(proj3d, proj3d, vt3d)


def stick_breaking_layer(x, norm_g, w_in, w_out):
    b, s, d = x.shape
    x2d = x.reshape(b * s, d)
    w_in_l = w_in.at[:, :SB_WIDTH].multiply(LOG2E / math.sqrt(SB_HEAD_DIM))
    proj, vt = norm_matmul(x2d, norm_g, w_in_l.astype(jnp.bfloat16), seq=s,
                           vt_cols=(2 * SB_WIDTH, SB_WIDTH))
    o = sb_attention(proj.reshape(b, s, -1), vt)
    y = gate_out_proj(o.reshape(b * s, SB_WIDTH), proj, 3, x2d, w_out.astype(jnp.bfloat16))
    return y.reshape(b, s, d)


def _rope_tables(seq):
    half = MLA_ROPE_DIM // 2
    inv_freq = ROPE_THETA ** (-jnp.arange(half, dtype=jnp.float32) / half)
    ang = jnp.arange(seq, dtype=jnp.float32)[:, None] * inv_freq[None, :]
    cos, sin = jnp.cos(ang), jnp.sin(ang)
    zero = jnp.zeros((seq, LANES - MLA_ROPE_DIM), jnp.float32)
    return (jnp.concatenate([cos, cos, zero], axis=1),
            jnp.concatenate([-sin, sin, zero], axis=1))


def _rotate_half(pe):
    lane = lax.broadcasted_iota(jnp.int32, pe.shape, 1)
    half = MLA_ROPE_DIM // 2
    return jnp.where(lane < half, pltpu.roll(pe, LANES - half, axis=1), pltpu.roll(pe, half, axis=1))


def _mla_prep_kernel(qlat_ref, kvl_ref, cos_ref, sin_ref, qa_ref, kva_ref, gq_ref, gk_ref,
                     wuq_ref, wukv_ref, q_ref, k_ref, vt_ref):
    cos, sin = cos_ref[...], sin_ref[...]
    ql = qlat_ref[...].astype(jnp.float32)
    qn = (ql * _rms_scale(ql, MLA_Q_RANK) * qa_ref[...]).astype(jnp.bfloat16)
    kvl = kvl_ref[...].astype(jnp.float32)
    kv_lat = kvl[:, :MLA_KV_RANK]
    kvn = (kv_lat * _rms_scale(kv_lat, MLA_KV_RANK) * kva_ref[...]).astype(jnp.bfloat16)
    gq, gk = gq_ref[...], gk_ref[...]
    k_pe = kvl[:, MLA_KV_RANK:]
    k_pe_sq = jnp.sum(k_pe * k_pe, axis=-1, keepdims=True)
    k_pe_g = k_pe * gk[:, MLA_NOPE_DIM:]
    k_pe_rot = k_pe_g * cos + _rotate_half(k_pe_g) * sin
    for h in range(MLA_HEADS):
        c0 = h * MLA_QK_PAD
        qh = jnp.dot(qn, wuq_ref[:, c0:c0 + MLA_QK_PAD], preferred_element_type=jnp.float32)
        q_pe = qh[:, MLA_NOPE_DIM:]
        q_nope = qh[:, :MLA_NOPE_DIM]
        q_ss = jnp.sum(q_nope * q_nope + 0.5 * (q_pe * q_pe), axis=-1, keepdims=True)
        r = lax.rsqrt(q_ss * (1.0 / MLA_QK_DIM) + NORM_EPS)
        q_ref[:, c0:c0 + MLA_NOPE_DIM] = (q_nope * r * gq[:, :MLA_NOPE_DIM]).astype(q_ref.dtype)
        q_pe = q_pe * r * gq[:, MLA_NOPE_DIM:]
        q_ref[:, c0 + MLA_NOPE_DIM:c0 + MLA_QK_PAD] = (
            q_pe * cos + pltpu.roll(q_pe, MLA_ROPE_DIM // 2, axis=1) * sin).astype(q_ref.dtype)
        kvh = jnp.dot(kvn, wukv_ref[:, c0:c0 + MLA_QK_PAD], preferred_element_type=jnp.float32)
        k_nope = kvh[:, :MLA_NOPE_DIM]
        r = lax.rsqrt((jnp.sum(k_nope * k_nope, axis=-1, keepdims=True) + k_pe_sq)
                      * (1.0 / MLA_QK_DIM) + NORM_EPS)
        k_ref[:, c0:c0 + MLA_NOPE_DIM] = (k_nope * r * gk[:, :MLA_NOPE_DIM]).astype(k_ref.dtype)
        k_ref[:, c0 + MLA_NOPE_DIM:c0 + MLA_QK_PAD] = (k_pe_rot * r).astype(k_ref.dtype)
        vt_ref[0, h * MLA_V_DIM:(h + 1) * MLA_V_DIM, :] = kvh[:, MLA_NOPE_DIM:].T.astype(vt_ref.dtype)


def mla_prep(proj2d, seq, q_a_norm, kv_a_norm, q_head_norm, k_head_norm, wuq_pad, wukv, *, tm=512):
    n = proj2d.shape[0]
    cos, sin = _rope_tables(seq)
    pad = jnp.zeros((MLA_QK_PAD - MLA_QK_DIM,), jnp.float32)
    gq = q_head_norm * (LOG2E / math.sqrt(MLA_QK_DIM))
    gq = jnp.concatenate([gq, gq[MLA_NOPE_DIM:]]).reshape(1, MLA_QK_PAD)
    gk = jnp.concatenate([k_head_norm, pad]).reshape(1, MLA_QK_PAD)
    qlat_blk = MLA_WIDTH // MLA_Q_RANK
    tiles_per_seq = seq // tm
    row = lambda i: (i, 0)
    const = lambda i: (0, 0)
    pos = lambda i: (i % tiles_per_seq, 0)
    return pl.pallas_call(
        _mla_prep_kernel,
        grid=(n // tm,),
        in_specs=[pl.BlockSpec((tm, MLA_Q_RANK), lambda i: (i, qlat_blk)),
                  pl.BlockSpec((tm, MLA_Q_RANK), lambda i: (i, qlat_blk + 1)),
                  pl.BlockSpec((tm, LANES), pos),
                  pl.BlockSpec((tm, LANES), pos),
                  pl.BlockSpec((1, MLA_Q_RANK), const),
                  pl.BlockSpec((1, MLA_KV_RANK), const),
                  pl.BlockSpec((1, MLA_QK_PAD), const),
                  pl.BlockSpec((1, MLA_QK_PAD), const),
                  pl.BlockSpec((MLA_Q_RANK, MLA_HEADS * MLA_QK_PAD), const),
                  pl.BlockSpec((MLA_KV_RANK, MLA_HEADS * MLA_QK_PAD), const)],
        out_specs=[pl.BlockSpec((tm, MLA_HEADS * MLA_QK_PAD), row),
                   pl.BlockSpec((tm, MLA_HEADS * MLA_QK_PAD), row),
                   pl.BlockSpec((1, MLA_WIDTH, tm),
                                lambda i: (i // tiles_per_seq, 0, i % tiles_per_seq))],
        out_shape=[jax.ShapeDtypeStruct((n, MLA_HEADS * MLA_QK_PAD), jnp.bfloat16),
                   jax.ShapeDtypeStruct((n, MLA_HEADS * MLA_QK_PAD), jnp.bfloat16),
                   jax.ShapeDtypeStruct((n // seq, MLA_WIDTH, seq), jnp.bfloat16)],
        compiler_params=_cparams("arbitrary"),
        name="mla_prep",
    )(proj2d, proj2d, cos, sin, q_a_norm.reshape(1, -1), kv_a_norm.reshape(1, -1), gq, gk,
      wuq_pad, wukv)


def _causal_attn_kernel(q_ref, k_ref, vt_ref, o_ref, s_ref, p_ref, *, tile):
    seq = q_ref.shape[1]
    chains = range(2)
    group = 2 * tile
    causal = (lax.broadcasted_iota(jnp.int32, (tile, tile), 0)
              <= lax.broadcasted_iota(jnp.int32, (tile, tile), 1))

    def qk(q, s0):
        return lax.dot_general(k_ref[0, pl.ds(pl.multiple_of(s0, tile), tile), :], q,
                               (((1,), (1,)), ((), ())), preferred_element_type=jnp.float32)

    def pv(p, s0):
        return jnp.dot(vt_ref[0, :, pl.ds(pl.multiple_of(s0, tile), tile)], p,
                       preferred_element_type=jnp.float32)

    def softmax_step(s2, m, l, mask):
        if mask is not None:
            s2 = jnp.where(mask, s2, -jnp.inf)
        m_new = jnp.maximum(m, jnp.max(s2, axis=0, keepdims=True))
        p = jnp.exp2(s2 - m_new)
        alpha = jnp.exp2(m - m_new)
        return m_new, alpha * l + jnp.sum(p, axis=0, keepdims=True), alpha, p.astype(jnp.bfloat16)

    def start_group(t0):
        for c in chains:
            r0 = t0 + c * tile
            r0 = r0 if isinstance(r0, int) else pl.multiple_of(r0, tile)
            s_ref[c] = qk(q_ref[0, pl.ds(r0, tile), :], 0)

    for c in chains:
        p_ref[c] = jnp.zeros((tile, tile), p_ref.dtype)
    start_group(0)

    def q_group(i, _):
        t0 = pl.multiple_of(i * group, group)
        q = [q_ref[0, pl.ds(t0 + c * tile, tile), :] for c in chains]
        init = tuple((jnp.full((1, tile), -jnp.inf, jnp.float32), jnp.zeros((1, tile), jnp.float32),
                      jnp.zeros((vt_ref.shape[1], tile), jnp.float32),
                      jnp.ones((1, tile), jnp.float32)) for _ in chains)

        def first_half(j, state, s_b_chains, masks):
            b_prev = jnp.maximum(j - 1, 0) * group + tile
            pv_b = [pv(p_ref[c], b_prev) for c in chains]
            s_b = {c: qk(q[c], j * group + tile) for c in s_b_chains}
            out = []
            for c in chains:
                m, l, acc, alpha_b = state[c]
                acc = alpha_b * acc + jnp.where(j > 0, pv_b[c], 0.0)
                m, l, alpha_a, p_a = softmax_step(s_ref[c], m, l, masks[c])
                out.append((m, l, acc, alpha_a, p_a))
            return out, s_b

        def full_tiles(j, state):
            mid, s_b = first_half(j, state, chains, (None, None))
            pv_a = [pv(mid[c][4], j * group) for c in chains]
            for c in chains:
                s_ref[c] = qk(q[c], (j + 1) * group)
            out = []
            for c in chains:
                m, l, acc, alpha_a, _ = mid[c]
                acc = alpha_a * acc + pv_a[c]
                m, l, alpha_b, p_b = softmax_step(s_b[c], m, l, None)
                p_ref[c] = p_b
                out.append((m, l, acc, alpha_b))
            return tuple(out)

        state = lax.fori_loop(0, i, full_tiles, init)
        mid, s_b = first_half(i, state, (1,), (causal, None))
        start_group(jnp.minimum(t0 + group, seq - group))
        pv_a = [pv(mid[c][4], t0) for c in chains]
        m, l, acc, alpha_a, _ = mid[0]
        o_ref[0, pl.ds(t0, tile), :] = ((alpha_a * acc + pv_a[0]) / l).T.astype(o_ref.dtype)
        m, l, acc, alpha_a, _ = mid[1]
        acc = alpha_a * acc + pv_a[1]
        m, l, alpha_b, p_b = softmax_step(s_b[1], m, l, causal)
        acc = alpha_b * acc + pv(p_b, t0 + tile)
        o_ref[0, pl.ds(t0 + tile, tile), :] = (acc / l).T.astype(o_ref.dtype)
        return 0

    lax.fori_loop(0, seq // group, q_group, 0)


def mla_attention(q3d, k3d, vt3d, *, tile=1024):
    b, s, _ = q3d.shape
    qk = lambda bi, h: (bi, 0, h)
    return pl.pallas_call(
        functools.partial(_causal_attn_kernel, tile=tile),
        grid=(b, MLA_HEADS),
        in_specs=[pl.BlockSpec((1, s, MLA_QK_PAD), qk),
                  pl.BlockSpec((1, s, MLA_QK_PAD), qk),
                  pl.BlockSpec((1, MLA_V_DIM, s), lambda bi, h: (bi, h, 0))],
        out_specs=pl.BlockSpec((1, s, MLA_V_DIM), qk),
        out_shape=jax.ShapeDtypeStruct((b, s, MLA_WIDTH), jnp.bfloat16),
        scratch_shapes=[pltpu.VMEM((2, tile, tile), jnp.float32),
                        pltpu.VMEM((2, tile, tile), jnp.bfloat16)],
        compiler_params=_cparams("arbitrary", "arbitrary"),
        name="mla_attention",
    )(q3d, k3d, vt3d)


def mla_layer(x, norm_g, w_in, q_a_norm, w_uq, kv_a_norm, w_ukv, q_head_norm, k_head_norm, w_out):
    b, s, d = x.shape
    x2d = x.reshape(b * s, d)
    c1 = MLA_Q_RANK
    c3 = c1 + MLA_KV_RANK + MLA_ROPE_DIM
    w_in_l = jnp.concatenate(
        [w_in[:, c3:], w_in[:, :c3], jnp.zeros((d, LANES - MLA_ROPE_DIM), w_in.dtype)], axis=1)
    wuq3 = w_uq.reshape(MLA_Q_RANK, MLA_HEADS, MLA_QK_DIM)
    wuq_pad = jnp.concatenate([wuq3, wuq3[:, :, MLA_NOPE_DIM:]], axis=2).reshape(MLA_Q_RANK, -1)
    proj, = norm_matmul(x2d, norm_g, w_in_l.astype(jnp.bfloat16))
    q, k, vt = mla_prep(proj, s, q_a_norm, kv_a_norm, q_head_norm, k_head_norm,
                        wuq_pad.astype(jnp.bfloat16), w_ukv.astype(jnp.bfloat16))
    o = mla_attention(q.reshape(b, s, -1), k.reshape(b, s, -1), vt)
    y = gate_out_proj(o.reshape(b * s, MLA_WIDTH), proj, 0, x2d, w_out.astype(jnp.bfloat16))
    return y.reshape(b, s, d)


def _swa_attn_kernel(q_ref, k_ref, v_ref, slope_ref, sink_ref, o_ref, kn_ref, vt_ref, bias_ref,
                     s_ref, *, tq):
    seq = q_ref.shape[1]
    ncol = q_ref.shape[2] // LANES
    halves = range(LANES // SWA_HEAD_DIM)
    col_pairs = range(ncol // 2)
    tk = tq + SWA_WINDOW
    pair = pl.program_id(1)
    kn_ref[pl.ds(0, SWA_WINDOW), :] = jnp.zeros((SWA_WINDOW, LANES), kn_ref.dtype)
    kn_ref[pl.ds(SWA_WINDOW, seq), :] = k_ref[0]
    vt_ref[:, pl.ds(0, SWA_WINDOW)] = jnp.zeros((LANES, SWA_WINDOW), vt_ref.dtype)
    vt_ref[:, pl.ds(SWA_WINDOW, seq)] = v_ref[0]
    lane = lax.broadcasted_iota(jnp.int32, (1, LANES), 1)
    in_half = [(lane >= h * SWA_HEAD_DIM) & (lane < (h + 1) * SWA_HEAD_DIM) for h in halves]
    key_row = lax.broadcasted_iota(jnp.int32, (tk, tq), 0)
    rel = lax.broadcasted_iota(jnp.int32, (tk, tq), 1) + SWA_WINDOW - key_row
    in_window = (rel >= 0) & (rel < SWA_WINDOW)
    rel_f = rel.astype(jnp.float32)
    for h in halves:
        for c in range(ncol):
            bias_ref[h, :, c * tq:(c + 1) * tq] = jnp.where(
                in_window, -slope_ref[pair, h, c] * rel_f, -jnp.inf)
    sink_row = [[jnp.concatenate([jnp.full((1, tq), sink_ref[pair, h, 2 * cp + cc], jnp.float32)
                                  for cc in range(2)], axis=1) for cp in col_pairs] for h in halves]
    before_start = jnp.concatenate([key_row < SWA_WINDOW] * 2, axis=1)

    chains = [(h, cp) for h in halves for cp in col_pairs]

    def scores(t0):
        kb = kn_ref[pl.ds(t0, tk), :]
        qn = [q_ref[0, pl.ds(t0, tq), c * LANES:(c + 1) * LANES] for c in range(ncol)]
        return [lax.dot_general(
            kb, jnp.concatenate([jnp.where(in_half[h], qn[2 * cp + cc], jnp.zeros_like(qn[0]))
                                 for cc in range(2)], axis=0),
            (((1,), (1,)), ((), ())), preferred_element_type=jnp.float32) for h, cp in chains]

    for idx, s0 in enumerate(scores(0)):
        s_ref[idx] = s0

    def q_tile(i, first):
        t0 = pl.multiple_of(i * tq, tq)
        s_next = scores(pl.multiple_of(jnp.minimum(t0 + tq, seq - tq), tq))
        o_t = {}
        for idx, (h, cp) in enumerate(chains):
            s = s_ref[idx] + bias_ref[h, :, 2 * cp * tq:2 * (cp + 1) * tq]
            s_ref[idx] = s_next[idx]
            if first:
                s = jnp.where(before_start, -jnp.inf, s)
            m = jnp.maximum(jnp.max(s, axis=0, keepdims=True), sink_row[h][cp])
            e = jnp.exp(s - m)
            denom = jnp.sum(e, axis=0, keepdims=True) + jnp.exp(sink_row[h][cp] - m)
            pv = jnp.dot(vt_ref[h * SWA_HEAD_DIM:(h + 1) * SWA_HEAD_DIM, pl.ds(t0, tk)],
                         e.astype(jnp.bfloat16), preferred_element_type=jnp.float32)
            o_t[h, cp] = pv / denom
        for c in range(ncol):
            cp, cc = divmod(c, 2)
            o_col_t = jnp.concatenate([o_t[h, cp][:, cc * tq:(cc + 1) * tq] for h in halves], axis=0)
            o_ref[0, pl.ds(t0, tq), c * LANES:(c + 1) * LANES] = o_col_t.T.astype(o_ref.dtype)

    q_tile(0, True)

    def q_tile_loop(i, _):
        q_tile(i, False)
        return 0

    lax.fori_loop(1, seq // tq, q_tile_loop, 0)


def swa_attention(proj3d, vt3d, slopes, sinks, *, tq=128):
    b, s, _ = proj3d.shape
    npair = SWA_KV_HEADS // 2
    qw = SWA_WIDTH // npair
    kcol0 = 2 * SWA_WIDTH // LANES
    smem = pl.BlockSpec(memory_space=pltpu.SMEM)
    return pl.pallas_call(
        functools.partial(_swa_attn_kernel, tq=tq),
        grid=(b, npair),
        in_specs=[pl.BlockSpec((1, s, qw), lambda bi, p: (bi, 0, p)),
                  pl.BlockSpec((1, s, LANES), lambda bi, p: (bi, 0, kcol0 + p)),
                  pl.BlockSpec((1, LANES, s), lambda bi, p: (bi, p, 0)),
                  smem, smem],
        out_specs=pl.BlockSpec((1, s, qw), lambda bi, p: (bi, 0, p)),
        out_shape=jax.ShapeDtypeStruct((b, s, SWA_WIDTH), jnp.bfloat16),
        scratch_shapes=[pltpu.VMEM((s + SWA_WINDOW, LANES), jnp.bfloat16),
                        pltpu.VMEM((LANES, s + SWA_WINDOW), jnp.bfloat16),
                        pltpu.VMEM((LANES // SWA_HEAD_DIM, tq + SWA_WINDOW, qw // LANES * tq),
                                   jnp.float32),
                        pltpu.VMEM((qw // LANES, tq + SWA_WINDOW, 2 * tq), jnp.float32)],
        compiler_params=_cparams("arbitrary", "arbitrary"),
        name="swa_attention",
    )(proj3d, proj3d, vt3d, slopes, sinks)


def _swa_head_order():
    order = []
    for p in range(SWA_KV_HEADS // 2):
        for c in range(SWA_GROUP):
            order += [2 * p * SWA_GROUP + c, (2 * p + 1) * SWA_GROUP + c]
    return order


def swa_layer(x, norm_g, w_in, q_head_norm, k_head_norm, sinks, w_out):
    b, s, d = x.shape
    x2d = x.reshape(b * s, d)
    kv_w = SWA_KV_HEADS * SWA_HEAD_DIM
    c1, c2, c3 = SWA_WIDTH, SWA_WIDTH + kv_w, SWA_WIDTH + 2 * kv_w
    order = jnp.asarray(_swa_head_order())
    perm = (order[:, None] * SWA_HEAD_DIM + jnp.arange(SWA_HEAD_DIM)[None, :]).reshape(-1)
    w_in_l = jnp.concatenate([w_in[:, :c1][:, perm], w_in[:, c3:][:, perm], w_in[:, c1:c3]], axis=1)
    w_out_l = w_out[perm, :]
    slopes = 2.0 ** (-8.0 * jnp.arange(1, SWA_HEADS + 1, dtype=jnp.float32) / SWA_HEADS)
    by_pos = lambda t: t[order].reshape(SWA_KV_HEADS // 2, SWA_GROUP, 2).transpose(0, 2, 1)
    q_cols, k_cols = (0, SWA_WIDTH), (2 * SWA_WIDTH, 2 * SWA_WIDTH + kv_w)
    head_gain = jnp.zeros((1, w_in_l.shape[1]), jnp.float32)
    head_gain = head_gain.at[0, q_cols[0]:q_cols[1]].set(
        jnp.tile(q_head_norm, SWA_HEADS) / math.sqrt(SWA_HEAD_DIM))
    head_gain = head_gain.at[0, k_cols[0]:k_cols[1]].set(jnp.tile(k_head_norm, SWA_KV_HEADS))
    proj, vt = norm_matmul(x2d, norm_g, w_in_l.astype(jnp.bfloat16), seq=s,
                           vt_cols=(2 * SWA_WIDTH + kv_w, kv_w),
                           head_gain=head_gain, head_cols=(q_cols, k_cols))
    o = swa_attention(proj.reshape(b, s, -1), vt, by_pos(slopes), by_pos(sinks.astype(jnp.float32)))
    y = gate_out_proj(o.reshape(b * s, SWA_WIDTH), proj, 1, x2d, w_out_l.astype(jnp.bfloat16))
    return y.reshape(b, s, d)


def kernel(x, l0_norm, l0_w_in, l0_w_out, l1_norm, l1_w_in, l1_q_a_norm, l1_w_uq, l1_kv_a_norm,
           l1_w_ukv, l1_q_head_norm, l1_k_head_norm, l1_w_out, l2_norm, l2_w_in, l2_q_head_norm,
           l2_k_head_norm, l2_sinks, l2_w_out, l3_norm, l3_w_in, l3_w_out):
    x = stick_breaking_layer(x, l0_norm, l0_w_in, l0_w_out)
    x = mla_layer(x, l1_norm, l1_w_in, l1_q_a_norm, l1_w_uq, l1_kv_a_norm, l1_w_ukv,
                  l1_q_head_norm, l1_k_head_norm, l1_w_out)
    x = swa_layer(x, l2_norm, l2_w_in, l2_q_head_norm, l2_k_head_norm, l2_sinks, l2_w_out)
    x = stick_breaking_layer(x, l3_norm, l3_w_in, l3_w_out)
    return x
```

```python
import functools
import math

import jax
import jax.numpy as jnp
from jax import lax
from jax.experimental import pallas as pl
from jax.experimental.pallas import tpu as pltpu

NORM_EPS = 1e-6
LOG2E = 1.4426950408889634

LANES = 128
VMEM_LIMIT_BYTES = 56 * 1024 * 1024

SB_HEADS = 16
SB_HEAD_DIM = 64
SB_WIDTH = SB_HEADS * SB_HEAD_DIM

MLA_HEADS = 8
MLA_NOPE_DIM = 128
MLA_ROPE_DIM = 64
MLA_QK_DIM = MLA_NOPE_DIM + MLA_ROPE_DIM
MLA_QK_PAD = 256
MLA_V_DIM = 128
MLA_Q_RANK = 256
MLA_KV_RANK = 128
MLA_WIDTH = MLA_HEADS * MLA_V_DIM
ROPE_THETA = 10000.0

SWA_HEADS = 16
SWA_KV_HEADS = 4
SWA_GROUP = SWA_HEADS // SWA_KV_HEADS
SWA_HEAD_DIM = 64
SWA_WINDOW = 128
SWA_WIDTH = SWA_HEADS * SWA_HEAD_DIM

F32_EXP2_ZERO_BELOW = -160.0
MASKED_LOGIT = 1e30


def _cparams(*sem):
    return pltpu.CompilerParams(dimension_semantics=sem, vmem_limit_bytes=VMEM_LIMIT_BYTES)


def _rms_scale(xf, n):
    return lax.rsqrt(jnp.sum(xf * xf, axis=-1, keepdims=True) * (1.0 / n) + NORM_EPS)


def _halves_rms(xf, gain):
    lane = lax.broadcasted_iota(jnp.int32, xf.shape, 1)
    lo = lane < LANES // 2
    sq = xf * xf
    s_lo = jnp.sum(jnp.where(lo, sq, 0.0), axis=-1, keepdims=True)
    s_hi = jnp.sum(jnp.where(lo, 0.0, sq), axis=-1, keepdims=True)
    r = lax.rsqrt(jnp.where(lo, s_lo, s_hi) * (2.0 / LANES) + NORM_EPS)
    return xf * r * gain


def _norm_matmul_kernel(x_ref, g_ref, w_ref, *refs, n_chunk, vt_cols, head_cols):
    refs = list(refs)
    hg_ref = refs.pop(0) if head_cols else None
    o_ref = refs.pop(0)
    vt_ref = refs.pop(0) if vt_cols is not None else None
    xf = x_ref[...]
    xn = (xf * _rms_scale(xf, xf.shape[-1]) * g_ref[...]).astype(jnp.bfloat16)
    n_out = o_ref.shape[-1]
    for c in range(0, n_out, n_chunk):
        acc = jnp.dot(xn, w_ref[:, c:c + n_chunk], preferred_element_type=jnp.float32)
        if any(lo < c + n_chunk and c < hi for lo, hi in head_cols):
            acc = jnp.concatenate(
                [_halves_rms(acc[:, j:j + LANES], hg_ref[:, c + j:c + j + LANES])
                 if any(lo <= c + j < hi for lo, hi in head_cols) else acc[:, j:j + LANES]
                 for j in range(0, n_chunk, LANES)], axis=1)
        o_ref[:, c:c + n_chunk] = acc.astype(o_ref.dtype)
        if vt_cols is not None:
            lo, hi = max(c, vt_cols[0]), min(c + n_chunk, vt_cols[0] + vt_cols[1])
            if lo < hi:
                vt_ref[0, lo - vt_cols[0]:hi - vt_cols[0], :] = (
                    acc[:, lo - c:hi - c].T.astype(vt_ref.dtype))


def norm_matmul(x2d, g, w_bf16, *, seq=None, vt_cols=None, head_gain=None, head_cols=(),
                tm=512, n_chunk=512):
    n, d = x2d.shape
    n_out = w_bf16.shape[1]
    in_specs = [pl.BlockSpec((tm, d), lambda i: (i, 0)),
                pl.BlockSpec((1, d), lambda i: (0, 0)),
                pl.BlockSpec((d, n_out), lambda i: (0, 0))]
    args = [x2d, g.reshape(1, d), w_bf16]
    if head_cols:
        in_specs.append(pl.BlockSpec((1, n_out), lambda i: (0, 0)))
        args.append(head_gain)
    out_specs = [pl.BlockSpec((tm, n_out), lambda i: (i, 0))]
    out_shape = [jax.ShapeDtypeStruct((n, n_out), jnp.bfloat16)]
    if vt_cols is not None:
        tiles_per_seq = seq // tm
        out_specs.append(pl.BlockSpec((1, vt_cols[1], tm),
                                      lambda i: (i // tiles_per_seq, 0, i % tiles_per_seq)))
        out_shape.append(jax.ShapeDtypeStruct((n // seq, vt_cols[1], seq), jnp.bfloat16))
    return pl.pallas_call(
        functools.partial(_norm_matmul_kernel, n_chunk=n_chunk, vt_cols=vt_cols,
                          head_cols=tuple(head_cols)),
        grid=(n // tm,),
        in_specs=in_specs,
        out_specs=out_specs,
        out_shape=out_shape,
        compiler_params=_cparams("arbitrary"),
        name="norm_matmul",
    )(*args)


def _gate_out_proj_kernel(o_ref, gate_ref, x_ref, w_ref, y_ref):
    gate = gate_ref[...].astype(jnp.float32)
    og = (o_ref[...].astype(jnp.float32) * (gate * jax.nn.sigmoid(gate))).astype(jnp.bfloat16)
    y_ref[...] = x_ref[...] + jnp.dot(og, w_ref[...], preferred_element_type=jnp.float32)


def gate_out_proj(o2d, proj2d, gate_block, x2d, w_bf16, *, tm=512):
    n, width = o2d.shape
    d = x2d.shape[1]
    return pl.pallas_call(
        _gate_out_proj_kernel,
        grid=(n // tm,),
        in_specs=[pl.BlockSpec((tm, width), lambda i: (i, 0)),
                  pl.BlockSpec((tm, width), lambda i: (i, gate_block)),
                  pl.BlockSpec((tm, d), lambda i: (i, 0)),
                  pl.BlockSpec((width, d), lambda i: (0, 0))],
        out_specs=pl.BlockSpec((tm, d), lambda i: (i, 0)),
        out_shape=jax.ShapeDtypeStruct((n, d), jnp.float32),
        compiler_params=_cparams("arbitrary"),
        name="gate_out_proj",
    )(o2d, proj2d, x2d, w_bf16)


def _sb_scores(k_blk, qm):
    return [lax.dot_general(k_blk, q_h, (((1,), (1,)), ((), ())), preferred_element_type=jnp.float32)
            for q_h in qm]


def _sb_front(z, mask, carry, tri):
    z2 = z
    if mask is not None:
        z2 = jnp.where(mask, z2, -MASKED_LOGIT)
    soft = jnp.log2(1.0 + jnp.exp2(-jnp.abs(z2)))
    ls2 = jnp.minimum(z2, 0.0) - soft
    lf2 = ls2 - z2
    later = jnp.dot(tri, lf2.astype(jnp.bfloat16), preferred_element_type=jnp.float32)
    return (ls2 if carry is None else ls2 + carry), later, jnp.sum(lf2, axis=0, keepdims=True)


def _sb_back(arg, later, vt_blk):
    return jnp.dot(vt_blk, jnp.exp2(arg + later).astype(jnp.bfloat16),
                   preferred_element_type=jnp.float32)


def _sb_pass(q_ref, k_ref, vt_ref, o_ref, arg_ref, later_ref, *, tile, head_dim, pair, with_sweeps):
    seq = q_ref.shape[1]
    n_groups = seq // (pair * tile)
    heads = range(LANES // head_dim)
    lane = lax.broadcasted_iota(jnp.int32, (1, LANES), 1)
    in_head = [(lane >= h * head_dim) & (lane < (h + 1) * head_dim) for h in heads]
    earlier = (lax.broadcasted_iota(jnp.int32, (tile, tile), 0)
               < lax.broadcasted_iota(jnp.int32, (tile, tile), 1))
    tri = earlier.astype(jnp.bfloat16)
    masks = (earlier, None)

    def rows(s0):
        return pl.ds(s0 if isinstance(s0, int) else pl.multiple_of(s0, tile), tile)

    def masked_q(t0):
        q = q_ref[0, rows(t0), :]
        return [jnp.where(m, q, jnp.zeros_like(q)) for m in in_head]

    def vt_tile(s0, h):
        return vt_ref[0, h * head_dim:(h + 1) * head_dim, rows(s0)]

    def slot(p, kind, h):
        return (2 * p + kind) * len(heads) + h

    def before(t0, kind):
        s0 = t0 - kind * tile
        return max(s0, 0) if isinstance(s0, int) else jnp.maximum(s0, 0)

    def scores(g0, kinds_of):
        qm = [masked_q(g0 + p * tile) for p in range(pair)]
        return [[_sb_scores(k_ref[0, rows(before(g0 + p * tile, kind)), :], qm[p])
                 for kind in kinds_of[p]] for p in range(pair)]

    def park(z):
        carries = []
        for p in range(pair):
            carry = [None for _ in heads]
            for kind, z_kind in enumerate(z[p]):
                for h in heads:
                    arg, later, colsum = _sb_front(z_kind[h], masks[kind], carry[h], tri)
                    arg_ref[slot(p, kind, h)] = arg
                    later_ref[slot(p, kind, h)] = later
                    carry[h] = colsum if carry[h] is None else carry[h] + colsum
            carries.append(carry)
        return carries

    def finish(g0, acc_swept):
        for p in range(pair):
            t0 = g0 + p * tile
            acc = list(acc_swept[p])
            for kind in range(2):
                for h in heads:
                    acc[h] = acc[h] + _sb_back(arg_ref[slot(p, kind, h)], later_ref[slot(p, kind, h)],
                                               vt_tile(before(t0, kind), h))
            o_ref[0, rows(t0), :] = jnp.concatenate(acc, axis=0).T.astype(o_ref.dtype)

    zero_acc = tuple(jnp.zeros((head_dim, tile), jnp.float32) for _ in heads)

    def alive(carry):
        return jnp.max(functools.reduce(jnp.maximum, carry)) >= F32_EXP2_ZERO_BELOW

    def sweeps(g0, first_index, carries):
        can_sweep = [p for p in range(pair) if not (isinstance(first_index, int) and first_index + p < 2)]
        live_any = (alive([c for p in can_sweep for c in carries[p]]) if can_sweep
                    else jnp.bool_(False))
        if not with_sweeps:
            return (zero_acc,) * pair, live_any
        swept = []
        for p in range(pair):
            last = first_index + p
            if isinstance(last, int) and last < 2:
                swept.append(zero_acc)
                continue
            t0 = g0 + p * tile

            def cond(state, last=last):
                jj, live, _, _ = state
                return (jj <= last) & live

            def sweep(state, t0=t0):
                jj, _, acc, carry = state
                s0 = t0 - jj * tile
                z_s = _sb_scores(k_ref[0, rows(s0), :], masked_q(t0))
                acc, carry = list(acc), list(carry)
                for h in heads:
                    arg, later, colsum = _sb_front(z_s[h], None, carry[h], tri)
                    acc[h] = acc[h] + _sb_back(arg, later, vt_tile(s0, h))
                    carry[h] = carry[h] + colsum
                return jj + 1, alive(carry), tuple(acc), tuple(carry)

            _, _, acc, _ = lax.while_loop(cond, sweep, (jnp.int32(2), live_any, zero_acc,
                                                        tuple(carries[p])))
            swept.append(acc)
        return tuple(swept), live_any

    carries = park(scores(0, [(0,)] + [(0, 1)] * (pair - 1)))
    for h in heads:
        arg_ref[slot(0, 1, h)] = jnp.full((tile, tile), -MASKED_LOGIT, jnp.float32)
        later_ref[slot(0, 1, h)] = jnp.zeros((tile, tile), jnp.float32)
    acc_swept, live_first = sweeps(0, 0, carries)

    def group(g, state):
        acc_swept, live = state
        g0 = g * (pair * tile)
        z = scores(g0, [(0, 1)] * pair)
        finish(g0 - pair * tile, acc_swept)
        swept, live_g = sweeps(g0, g * pair, park(z))
        return swept, live | live_g

    acc_swept, live = lax.fori_loop(1, n_groups, group, (acc_swept, live_first))
    finish((n_groups - 1) * pair * tile, acc_swept)
    return live


def _sb_attn_kernel(*refs, tile, head_dim, pair):
    fast_pair = refs[0].shape[1] // tile // 2
    live = _sb_pass(*refs, tile=tile, head_dim=head_dim, pair=fast_pair, with_sweeps=False)

    @pl.when(live)
    def _():
        _sb_pass(*refs, tile=tile, head_dim=head_dim, pair=pair, with_sweeps=True)


def sb_attention(proj3d, vt3d, *, tile=256, pair=2):
    b, s, _ = proj3d.shape
    cols = SB_WIDTH // LANES
    blk = lambda off: pl.BlockSpec((1, s, LANES), lambda bi, p: (bi, 0, off + p))
    return pl.pallas_call(
        functools.partial(_sb_attn_kernel, tile=tile, head_dim=SB_HEAD_DIM, pair=pair),
        grid=(b, cols),
        in_specs=[blk(0), blk(cols), pl.BlockSpec((1, LANES, s), lambda bi, p: (bi, p, 0))],
        out_specs=pl.BlockSpec((1, s, LANES), lambda bi, p: (bi, 0, p)),
        out_shape=jax.ShapeDtypeStruct((b, s, SB_WIDTH), jnp.bfloat16),
        scratch_shapes=[pltpu.VMEM((s // tile * LANES // SB_HEAD_DIM, tile, tile), jnp.float32)] * 2,
        compiler_params=_cparams("arbitrary", "arbitrary"),
        name="sb_attention",
    )(proj3d, proj3d, vt3d)


def stick_breaking_layer(x, norm_g, w_in, w_out):
    b, s, d = x.shape
    x2d = x.reshape(b * s, d)
    w_in_l = w_in.at[:, :SB_WIDTH].multiply(LOG2E / math.sqrt(SB_HEAD_DIM))
    proj, vt = norm_matmul(x2d, norm_g, w_in_l.astype(jnp.bfloat16), seq=s,
                           vt_cols=(2 * SB_WIDTH, SB_WIDTH))
    o = sb_attention(proj.reshape(b, s, -1), vt)
    y = gate_out_proj(o.reshape(b * s, SB_WIDTH), proj, 3, x2d, w_out.astype(jnp.bfloat16))
    return y.reshape(b, s, d)


def _rope_tables(seq):
    half = MLA_ROPE_DIM // 2
    inv_freq = ROPE_THETA ** (-jnp.arange(half, dtype=jnp.float32) / half)
    ang = jnp.arange(seq, dtype=jnp.float32)[:, None] * inv_freq[None, :]
    cos, sin = jnp.cos(ang), jnp.sin(ang)
    zero = jnp.zeros((seq, LANES - MLA_ROPE_DIM), jnp.float32)
    return (jnp.concatenate([cos, cos, zero], axis=1),
            jnp.concatenate([-sin, sin, zero], axis=1))


def _rotate_half(pe):
    lane = lax.broadcasted_iota(jnp.int32, pe.shape, 1)
    half = MLA_ROPE_DIM // 2
    return jnp.where(lane < half, pltpu.roll(pe, LANES - half, axis=1), pltpu.roll(pe, half, axis=1))


def _mla_prep_kernel(qlat_ref, kvl_ref, cos_ref, sin_ref, qa_ref, kva_ref, gq_ref, gk_ref,
                     wuq_ref, wukv_ref, q_ref, k_ref, vt_ref):
    cos, sin = cos_ref[...], sin_ref[...]
    ql = qlat_ref[...].astype(jnp.float32)
    qn = (ql * _rms_scale(ql, MLA_Q_RANK) * qa_ref[...]).astype(jnp.bfloat16)
    kvl = kvl_ref[...].astype(jnp.float32)
    kv_lat = kvl[:, :MLA_KV_RANK]
    kvn = (kv_lat * _rms_scale(kv_lat, MLA_KV_RANK) * kva_ref[...]).astype(jnp.bfloat16)
    gq, gk = gq_ref[...], gk_ref[...]
    k_pe = kvl[:, MLA_KV_RANK:]
    k_pe_sq = jnp.sum(k_pe * k_pe, axis=-1, keepdims=True)
    k_pe_g = k_pe * gk[:, MLA_NOPE_DIM:]
    k_pe_rot = k_pe_g * cos + _rotate_half(k_pe_g) * sin
    for h in range(MLA_HEADS):
        c0 = h * MLA_QK_PAD
        qh = jnp.dot(qn, wuq_ref[:, c0:c0 + MLA_QK_PAD], preferred_element_type=jnp.float32)
        q_pe = qh[:, MLA_NOPE_DIM:]
        q_nope = qh[:, :MLA_NOPE_DIM]
        q_ss = jnp.sum(q_nope * q_nope + 0.5 * (q_pe * q_pe), axis=-1, keepdims=True)
        r = lax.rsqrt(q_ss * (1.0 / MLA_QK_DIM) + NORM_EPS)
        q_ref[:, c0:c0 + MLA_NOPE_DIM] = (q_nope * r * gq[:, :MLA_NOPE_DIM]).astype(q_ref.dtype)
        q_pe = q_pe * r * gq[:, MLA_NOPE_DIM:]
        q_ref[:, c0 + MLA_NOPE_DIM:c0 + MLA_QK_PAD] = (
            q_pe * cos + pltpu.roll(q_pe, MLA_ROPE_DIM // 2, axis=1) * sin).astype(q_ref.dtype)
        kvh = jnp.dot(kvn, wukv_ref[:, c0:c0 + MLA_QK_PAD], preferred_element_type=jnp.float32)
        k_nope = kvh[:, :MLA_NOPE_DIM]
        r = lax.rsqrt((jnp.sum(k_nope * k_nope, axis=-1, keepdims=True) + k_pe_sq)
                      * (1.0 / MLA_QK_DIM) + NORM_EPS)
        k_ref[:, c0:c0 + MLA_NOPE_DIM] = (k_nope * r * gk[:, :MLA_NOPE_DIM]).astype(k_ref.dtype)
        k_ref[:, c0 + MLA_NOPE_DIM:c0 + MLA_QK_PAD] = (k_pe_rot * r).astype(k_ref.dtype)
        vt_ref[0, h * MLA_V_DIM:(h + 1) * MLA_V_DIM, :] = kvh[:, MLA_NOPE_DIM:].T.astype(vt_ref.dtype)


def mla_prep(proj2d, seq, q_a_norm, kv_a_norm, q_head_norm, k_head_norm, wuq_pad, wukv, *, tm=512):
    n = proj2d.shape[0]
    cos, sin = _rope_tables(seq)
    pad = jnp.zeros((MLA_QK_PAD - MLA_QK_DIM,), jnp.float32)
    gq = q_head_norm * (LOG2E / math.sqrt(MLA_QK_DIM))
    gq = jnp.concatenate([gq, gq[MLA_NOPE_DIM:]]).reshape(1, MLA_QK_PAD)
    gk = jnp.concatenate([k_head_norm, pad]).reshape(1, MLA_QK_PAD)
    qlat_blk = MLA_WIDTH // MLA_Q_RANK
    tiles_per_seq = seq // tm
    row = lambda i: (i, 0)
    const = lambda i: (0, 0)
    pos = lambda i: (i % tiles_per_seq, 0)
    return pl.pallas_call(
        _mla_prep_kernel,
        grid=(n // tm,),
        in_specs=[pl.BlockSpec((tm, MLA_Q_RANK), lambda i: (i, qlat_blk)),
                  pl.BlockSpec((tm, MLA_Q_RANK), lambda i: (i, qlat_blk + 1)),
                  pl.BlockSpec((tm, LANES), pos),
                  pl.BlockSpec((tm, LANES), pos),
                  pl.BlockSpec((1, MLA_Q_RANK), const),
                  pl.BlockSpec((1, MLA_KV_RANK), const),
                  pl.BlockSpec((1, MLA_QK_PAD), const),
                  pl.BlockSpec((1, MLA_QK_PAD), const),
                  pl.BlockSpec((MLA_Q_RANK, MLA_HEADS * MLA_QK_PAD), const),
                  pl.BlockSpec((MLA_KV_RANK, MLA_HEADS * MLA_QK_PAD), const)],
        out_specs=[pl.BlockSpec((tm, MLA_HEADS * MLA_QK_PAD), row),
                   pl.BlockSpec((tm, MLA_HEADS * MLA_QK_PAD), row),
                   pl.BlockSpec((1, MLA_WIDTH, tm),
                                lambda i: (i // tiles_per_seq, 0, i % tiles_per_seq))],
        out_shape=[jax.ShapeDtypeStruct((n, MLA_HEADS * MLA_QK_PAD), jnp.bfloat16),
                   jax.ShapeDtypeStruct((n, MLA_HEADS * MLA_QK_PAD), jnp.bfloat16),
                   jax.ShapeDtypeStruct((n // seq, MLA_WIDTH, seq), jnp.bfloat16)],
        compiler_params=_cparams("arbitrary"),
        name="mla_prep",
    )(proj2d, proj2d, cos, sin, q_a_norm.reshape(1, -1), kv_a_norm.reshape(1, -1), gq, gk,
      wuq_pad, wukv)


def _causal_attn_kernel(q_ref, k_ref, vt_ref, o_ref, s_ref, p_ref, *, tile):
    seq = q_ref.shape[1]
    chains = range(2)
    group = 2 * tile
    causal = (lax.broadcasted_iota(jnp.int32, (tile, tile), 0)
              <= lax.broadcasted_iota(jnp.int32, (tile, tile), 1))

    def qk(q, s0):
        return lax.dot_general(k_ref[0, pl.ds(pl.multiple_of(s0, tile), tile), :], q,
                               (((1,), (1,)), ((), ())), preferred_element_type=jnp.float32)

    def pv(p, s0):
        return jnp.dot(vt_ref[0, :, pl.ds(pl.multiple_of(s0, tile), tile)], p,
                       preferred_element_type=jnp.float32)

    def softmax_step(s2, m, l, mask):
        if mask is not None:
            s2 = jnp.where(mask, s2, -jnp.inf)
        m_new = jnp.maximum(m, jnp.max(s2, axis=0, keepdims=True))
        p = jnp.exp2(s2 - m_new)
        alpha = jnp.exp2(m - m_new)
        return m_new, alpha * l + jnp.sum(p, axis=0, keepdims=True), alpha, p.astype(jnp.bfloat16)

    def start_group(t0):
        for c in chains:
            r0 = t0 + c * tile
            r0 = r0 if isinstance(r0, int) else pl.multiple_of(r0, tile)
            s_ref[c] = qk(q_ref[0, pl.ds(r0, tile), :], 0)

    for c in chains:
        p_ref[c] = jnp.zeros((tile, tile), p_ref.dtype)
    start_group(0)

    def q_group(i, _):
        t0 = pl.multiple_of(i * group, group)
        q = [q_ref[0, pl.ds(t0 + c * tile, tile), :] for c in chains]
        init = tuple((jnp.full((1, tile), -jnp.inf, jnp.float32), jnp.zeros((1, tile), jnp.float32),
                      jnp.zeros((vt_ref.shape[1], tile), jnp.float32),
                      jnp.ones((1, tile), jnp.float32)) for _ in chains)

        def first_half(j, state, s_b_chains, masks):
            b_prev = jnp.maximum(j - 1, 0) * group + tile
            pv_b = [pv(p_ref[c], b_prev) for c in chains]
            s_b = {c: qk(q[c], j * group + tile) for c in s_b_chains}
            out = []
            for c in chains:
                m, l, acc, alpha_b = state[c]
                acc = alpha_b * acc + jnp.where(j > 0, pv_b[c], 0.0)
                m, l, alpha_a, p_a = softmax_step(s_ref[c], m, l, masks[c])
                out.append((m, l, acc, alpha_a, p_a))
            return out, s_b

        def full_tiles(j, state):
            mid, s_b = first_half(j, state, chains, (None, None))
            pv_a = [pv(mid[c][4], j * group) for c in chains]
            for c in chains:
                s_ref[c] = qk(q[c], (j + 1) * group)
            out = []
            for c in chains:
                m, l, acc, alpha_a, _ = mid[c]
                acc = alpha_a * acc + pv_a[c]
                m, l, alpha_b, p_b = softmax_step(s_b[c], m, l, None)
                p_ref[c] = p_b
                out.append((m, l, acc, alpha_b))
            return tuple(out)

        state = lax.fori_loop(0, i, full_tiles, init)
        mid, s_b = first_half(i, state, (1,), (causal, None))
        start_group(jnp.minimum(t0 + group, seq - group))
        pv_a = [pv(mid[c][4], t0) for c in chains]
        m, l, acc, alpha_a, _ = mid[0]
        o_ref[0, pl.ds(t0, tile), :] = ((alpha_a * acc + pv_a[0]) / l).T.astype(o_ref.dtype)
        m, l, acc, alpha_a, _ = mid[1]
        acc = alpha_a * acc + pv_a[1]
        m, l, alpha_b, p_b = softmax_step(s_b[1], m, l, causal)
        acc = alpha_b * acc + pv(p_b, t0 + tile)
        o_ref[0, pl.ds(t0 + tile, tile), :] = (acc / l).T.astype(o_ref.dtype)
        return 0

    lax.fori_loop(0, seq // group, q_group, 0)


def mla_attention(q3d, k3d, vt3d, *, tile=1024):
    b, s, _ = q3d.shape
    qk = lambda bi, h: (bi, 0, h)
    return pl.pallas_call(
        functools.partial(_causal_attn_kernel, tile=tile),
        grid=(b, MLA_HEADS),
        in_specs=[pl.BlockSpec((1, s, MLA_QK_PAD), qk),
                  pl.BlockSpec((1, s, MLA_QK_PAD), qk),
                  pl.BlockSpec((1, MLA_V_DIM, s), lambda bi, h: (bi, h, 0))],
        out_specs=pl.BlockSpec((1, s, MLA_V_DIM), qk),
        out_shape=jax.ShapeDtypeStruct((b, s, MLA_WIDTH), jnp.bfloat16),
        scratch_shapes=[pltpu.VMEM((2, tile, tile), jnp.float32),
                        pltpu.VMEM((2, tile, tile), jnp.bfloat16)],
        compiler_params=_cparams("arbitrary", "arbitrary"),
        name="mla_attention",
    )(q3d, k3d, vt3d)


def mla_layer(x, norm_g, w_in, q_a_norm, w_uq, kv_a_norm, w_ukv, q_head_norm, k_head_norm, w_out):
    b, s, d = x.shape
    x2d = x.reshape(b * s, d)
    c1 = MLA_Q_RANK
    c3 = c1 + MLA_KV_RANK + MLA_ROPE_DIM
    w_in_l = jnp.concatenate(
        [w_in[:, c3:], w_in[:, :c3], jnp.zeros((d, LANES - MLA_ROPE_DIM), w_in.dtype)], axis=1)
    wuq3 = w_uq.reshape(MLA_Q_RANK, MLA_HEADS, MLA_QK_DIM)
    wuq_pad = jnp.concatenate([wuq3, wuq3[:, :, MLA_NOPE_DIM:]], axis=2).reshape(MLA_Q_RANK, -1)
    proj, = norm_matmul(x2d, norm_g, w_in_l.astype(jnp.bfloat16))
    q, k, vt = mla_prep(proj, s, q_a_norm, kv_a_norm, q_head_norm, k_head_norm,
                        wuq_pad.astype(jnp.bfloat16), w_ukv.astype(jnp.bfloat16))
    o = mla_attention(q.reshape(b, s, -1), k.reshape(b, s, -1), vt)
    y = gate_out_proj(o.reshape(b * s, MLA_WIDTH), proj, 0, x2d, w_out.astype(jnp.bfloat16))
    return y.reshape(b, s, d)


def _swa_attn_kernel(q_ref, k_ref, v_ref, slope_ref, sink_ref, o_ref, kn_ref, vt_ref, bias_ref,
                     s_ref, *, tq):
    seq = q_ref.shape[1]
    ncol = q_ref.shape[2] // LANES
    halves = range(LANES // SWA_HEAD_DIM)
    col_pairs = range(ncol // 2)
    tk = tq + SWA_WINDOW
    pair = pl.program_id(1)
    kn_ref[pl.ds(0, SWA_WINDOW), :] = jnp.zeros((SWA_WINDOW, LANES), kn_ref.dtype)
    kn_ref[pl.ds(SWA_WINDOW, seq), :] = k_ref[0]
    vt_ref[:, pl.ds(0, SWA_WINDOW)] = jnp.zeros((LANES, SWA_WINDOW), vt_ref.dtype)
    vt_ref[:, pl.ds(SWA_WINDOW, seq)] = v_ref[0]
    lane = lax.broadcasted_iota(jnp.int32, (1, LANES), 1)
    in_half = [(lane >= h * SWA_HEAD_DIM) & (lane < (h + 1) * SWA_HEAD_DIM) for h in halves]
    key_row = lax.broadcasted_iota(jnp.int32, (tk, tq), 0)
    rel = lax.broadcasted_iota(jnp.int32, (tk, tq), 1) + SWA_WINDOW - key_row
    in_window = (rel >= 0) & (rel < SWA_WINDOW)
    rel_f = rel.astype(jnp.float32)
    for h in halves:
        for c in range(ncol):
            bias_ref[h, :, c * tq:(c + 1) * tq] = jnp.where(
                in_window, -slope_ref[pair, h, c] * rel_f, -jnp.inf)
    sink_row = [[jnp.concatenate([jnp.full((1, tq), sink_ref[pair, h, 2 * cp + cc], jnp.float32)
                                  for cc in range(2)], axis=1) for cp in col_pairs] for h in halves]
    before_start = jnp.concatenate([key_row < SWA_WINDOW] * 2, axis=1)

    chains = [(h, cp) for h in halves for cp in col_pairs]

    def scores(t0):
        kb = kn_ref[pl.ds(t0, tk), :]
        qn = [q_ref[0, pl.ds(t0, tq), c * LANES:(c + 1) * LANES] for c in range(ncol)]
        return [lax.dot_general(
            kb, jnp.concatenate([jnp.where(in_half[h], qn[2 * cp + cc], jnp.zeros_like(qn[0]))
                                 for cc in range(2)], axis=0),
            (((1,), (1,)), ((), ())), preferred_element_type=jnp.float32) for h, cp in chains]

    for idx, s0 in enumerate(scores(0)):
        s_ref[idx] = s0

    def q_tile(i, first):
        t0 = pl.multiple_of(i * tq, tq)
        s_next = scores(pl.multiple_of(jnp.minimum(t0 + tq, seq - tq), tq))
        o_t = {}
        for idx, (h, cp) in enumerate(chains):
            s = s_ref[idx] + bias_ref[h, :, 2 * cp * tq:2 * (cp + 1) * tq]
            s_ref[idx] = s_next[idx]
            if first:
                s = jnp.where(before_start, -jnp.inf, s)
            m = jnp.maximum(jnp.max(s, axis=0, keepdims=True), sink_row[h][cp])
            e = jnp.exp(s - m)
            denom = jnp.sum(e, axis=0, keepdims=True) + jnp.exp(sink_row[h][cp] - m)
            pv = jnp.dot(vt_ref[h * SWA_HEAD_DIM:(h + 1) * SWA_HEAD_DIM, pl.ds(t0, tk)],
                         e.astype(jnp.bfloat16), preferred_element_type=jnp.float32)
            o_t[h, cp] = pv / denom
        for c in range(ncol):
            cp, cc = divmod(c, 2)
            o_col_t = jnp.concatenate([o_t[h, cp][:, cc * tq:(cc + 1) * tq] for h in halves], axis=0)
            o_ref[0, pl.ds(t0, tq), c * LANES:(c + 1) * LANES] = o_col_t.T.astype(o_ref.dtype)

    q_tile(0, True)

    def q_tile_loop(i, _):
        q_tile(i, False)
        return 0

    lax.fori_loop(1, seq // tq, q_tile_loop, 0)


def swa_attention(proj3d, vt3d, slopes, sinks, *, tq=128):
    b, s, _ = proj3d.shape
    npair = SWA_KV_HEADS // 2
    qw = SWA_WIDTH // npair
    kcol0 = 2 * SWA_WIDTH // LANES
    smem = pl.BlockSpec(memory_space=pltpu.SMEM)
    return pl.pallas_call(
        functools.partial(_swa_attn_kernel, tq=tq),
        grid=(b, npair),
        in_specs=[pl.BlockSpec((1, s, qw), lambda bi, p: (bi, 0, p)),
                  pl.BlockSpec((1, s, LANES), lambda bi, p: (bi, 0, kcol0 + p)),
                  pl.BlockSpec((1, LANES, s), lambda bi, p: (bi, p, 0)),
                  smem, smem],
        out_specs=pl.BlockSpec((1, s, qw), lambda bi, p: (bi, 0, p)),
        out_shape=jax.ShapeDtypeStruct((b, s, SWA_WIDTH), jnp.bfloat16),
        scratch_shapes=[pltpu.VMEM((s + SWA_WINDOW, LANES), jnp.bfloat16),
                        pltpu.VMEM((LANES, s + SWA_WINDOW), jnp.bfloat16),
                        pltpu.VMEM((LANES // SWA_HEAD_DIM, tq + SWA_WINDOW, qw // LANES * tq),
                                   jnp.float32),
                        pltpu.VMEM((qw // LANES, tq + SWA_WINDOW, 2 * tq), jnp.float32)],
        compiler_params=_cparams("arbitrary", "arbitrary"),
        name="swa_attention",
    )(proj3d, proj3d, vt3d, slopes, sinks)


def _swa_head_order():
    order = []
    for p in range(SWA_KV_HEADS // 2):
        for c in range(SWA_GROUP):
            order += [2 * p * SWA_GROUP + c, (2 * p + 1) * SWA_GROUP + c]
    return order


def swa_layer(x, norm_g, w_in, q_head_norm, k_head_norm, sinks, w_out):
    b, s, d = x.shape
    x2d = x.reshape(b * s, d)
    kv_w = SWA_KV_HEADS * SWA_HEAD_DIM
    c1, c2, c3 = SWA_WIDTH, SWA_WIDTH + kv_w, SWA_WIDTH + 2 * kv_w
    order = jnp.asarray(_swa_head_order())
    perm = (order[:, None] * SWA_HEAD_DIM + jnp.arange(SWA_HEAD_DIM)[None, :]).reshape(-1)
    w_in_l = jnp.concatenate([w_in[:, :c1][:, perm], w_in[:, c3:][:, perm], w_in[:, c1:c3]], axis=1)
    w_out_l = w_out[perm, :]
    slopes = 2.0 ** (-8.0 * jnp.arange(1, SWA_HEADS + 1, dtype=jnp.float32) / SWA_HEADS)
    by_pos = lambda t: t[order].reshape(SWA_KV_HEADS // 2, SWA_GROUP, 2).transpose(0, 2, 1)
    q_cols, k_cols = (0, SWA_WIDTH), (2 * SWA_WIDTH, 2 * SWA_WIDTH + kv_w)
    head_gain = jnp.zeros((1, w_in_l.shape[1]), jnp.float32)
    head_gain = head_gain.at[0, q_cols[0]:q_cols[1]].set(
        jnp.tile(q_head_norm, SWA_HEADS) / math.sqrt(SWA_HEAD_DIM))
    head_gain = head_gain.at[0, k_cols[0]:k_cols[1]].set(jnp.tile(k_head_norm, SWA_KV_HEADS))
    proj, vt = norm_matmul(x2d, norm_g, w_in_l.astype(jnp.bfloat16), seq=s,
                           vt_cols=(2 * SWA_WIDTH + kv_w, kv_w),
                           head_gain=head_gain, head_cols=(q_cols, k_cols))
    o = swa_attention(proj.reshape(b, s, -1), vt, by_pos(slopes), by_pos(sinks.astype(jnp.float32)))
    y = gate_out_proj(o.reshape(b * s, SWA_WIDTH), proj, 1, x2d, w_out_l.astype(jnp.bfloat16))
    return y.reshape(b, s, d)


def kernel(x, l0_norm, l0_w_in, l0_w_out, l1_norm, l1_w_in, l1_q_a_norm, l1_w_uq, l1_kv_a_norm,
           l1_w_ukv, l1_q_head_norm, l1_k_head_norm, l1_w_out, l2_norm, l2_w_in, l2_q_head_norm,
           l2_k_head_norm, l2_sinks, l2_w_out, l3_norm, l3_w_in, l3_w_out):
    x = stick_breaking_layer(x, l0_norm, l0_w_in, l0_w_out)
    x = mla_layer(x, l1_norm, l1_w_in, l1_q_a_norm, l1_w_uq, l1_kv_a_norm, l1_w_ukv,
                  l1_q_head_norm, l1_k_head_norm, l1_w_out)
    x = swa_layer(x, l2_norm, l2_w_in, l2_q_head_norm, l2_k_head_norm, l2_sinks, l2_w_out)
    x = stick_breaking_layer(x, l3_norm, l3_w_in, l3_w_out)
    return x
```
